```python
import jax, jax.numpy as jnp
from jax import lax
import numpy as np

D_MODEL = 1024
BATCH = 8
SEQ = 2048
DEPTH = 2

CHUNK = 64
Q_BLOCK = 128
CONV_WIDTH = 1024
CONV_GROUPS = 16
CONV_K = 3
MLA_HEADS = 16
QK_NOPE = 64
QK_ROPE = 32
V_DIM = 64
Q_LORA = 256
KV_LORA = 128
ROPE_THETA = 10000.0
N_BRANCH = 2
D_FF_DENSE = 2816
N_EXPERTS = 8
TOP_K = 2
D_FF_EXPERT = 1792
N_DENSE = (DEPTH + 1) // 2
N_MOE = DEPTH // 2
EPS = 1e-6
IN_SIZES = (CONV_WIDTH, CONV_WIDTH, CONV_WIDTH, Q_LORA, KV_LORA + QK_ROPE, N_BRANCH * D_MODEL)
IN_COLS = sum(IN_SIZES)

kernel_name = "hybrid_shortconv_mla_moe_trunk"


def rmsnorm(x, g):
    xf = x.astype(jnp.float32)
    inv = lax.rsqrt(jnp.mean(xf * xf, axis=-1, keepdims=True) + EPS)
    return (xf * inv).astype(x.dtype) * g


def rope_tables(positions):
    inv_freq = ROPE_THETA ** (-jnp.arange(0, QK_ROPE, 2, dtype=jnp.float32) / QK_ROPE)
    ang = positions.astype(jnp.float32)[..., None] * inv_freq
    return jnp.cos(ang), jnp.sin(ang)


def apply_rope(t, cos, sin):
    t1, t2 = jnp.split(t, 2, axis=-1)
    cos = cos.astype(t.dtype)
    sin = sin.astype(t.dtype)
    return jnp.concatenate([t1 * cos - t2 * sin, t1 * sin + t2 * cos], axis=-1)


def short_conv_mixer(gate_b, gate_c, u, conv_w, w_out):
    z = gate_c * u
    z = lax.conv_general_dilated(
        z, conv_w.astype(z.dtype), window_strides=(1,), padding=[(CONV_K - 1, 0)],
        dimension_numbers=("NWC", "WIO", "NWC"), feature_group_count=CONV_WIDTH)
    return (gate_b * z) @ w_out


def chunk_causal_attention(q_nope, q_rope, k_nope, k_rope, v):
    B, S, H, _ = q_nope.shape
    n_blk = S // Q_BLOCK
    scale = (QK_NOPE + QK_ROPE) ** -0.5
    k_chunk = jnp.arange(S) // CHUNK

    def to_blocks(t):
        return jnp.moveaxis(t.reshape((B, n_blk, Q_BLOCK) + t.shape[2:]), 1, 0)

    def one_block(args):
        blk, qn, qr = args
        s = (jnp.einsum("bqhd,bkhd->bhqk", qn, k_nope).astype(jnp.float32)
             + jnp.einsum("bqhr,bkr->bhqk", qr, k_rope).astype(jnp.float32)) * scale
        q_chunk = (blk * Q_BLOCK + jnp.arange(Q_BLOCK)) // CHUNK
        mask = k_chunk[None, :] <= q_chunk[:, None]
        s = jnp.where(mask[None, None], s, -jnp.inf)
        p = jax.nn.softmax(s, axis=-1).astype(v.dtype)
        return jnp.einsum("bhqk,bkhd->bqhd", p, v)

    o = lax.map(one_block, (jnp.arange(n_blk), to_blocks(q_nope), to_blocks(q_rope)))
    return jnp.moveaxis(o, 0, 1).reshape(B, S, H, V_DIM)


def mla_mixer(c_q, c_kv_full, cos, sin, g_q, g_kv, w_uq, w_ukv, w_o):
    B, S, _ = c_q.shape
    c_q = rmsnorm(c_q, g_q)
    q = (c_q @ w_uq).reshape(B, S, MLA_HEADS, QK_NOPE + QK_ROPE)
    q_nope = q[..., :QK_NOPE]
    q_rope = apply_rope(q[..., QK_NOPE:], cos[:, :, None, :], sin[:, :, None, :])
    c_kv = rmsnorm(c_kv_full[..., :KV_LORA], g_kv)
    k_rope = apply_rope(c_kv_full[..., KV_LORA:], cos, sin)
    kv = (c_kv @ w_ukv).reshape(B, S, MLA_HEADS, QK_NOPE + V_DIM)
    k_nope = kv[..., :QK_NOPE]
    v = kv[..., QK_NOPE:]
    o = chunk_causal_attention(q_nope, q_rope, k_nope, k_rope, v)
    return o.reshape(B, S, MLA_HEADS * V_DIM) @ w_o


def swiglu(t, w_gate, w_up, w_down):
    return (jax.nn.silu(t @ w_gate) * (t @ w_up)) @ w_down


def moe_swiglu(x, w_router, w_gate, w_up, w_down):
    B, S, D = x.shape
    t = x.reshape(B * S, D)
    logits = (t @ w_router).astype(jnp.float32)
    top_vals, top_idx = lax.top_k(logits, TOP_K)
    top_w = jax.nn.softmax(top_vals, axis=-1)
    gates = jnp.sum(jax.nn.one_hot(top_idx, N_EXPERTS, dtype=jnp.float32) * top_w[..., None], axis=1)
    out = jnp.zeros_like(t)
    for e in range(N_EXPERTS):
        out = out + gates[:, e:e + 1].astype(t.dtype) * swiglu(t, w_gate[e], w_up[e], w_down[e])
    return out.reshape(B, S, D)


def setup_inputs(seed: int = 0) -> dict:
    key = jax.random.key(seed)
    ks = jax.random.split(key, 32)

    def nrm(k, shape, fan_in):
        return jax.random.normal(k, shape, jnp.float32) * (fan_in ** -0.5)

    def gain(k, shape):
        return 1.0 + 0.05 * jax.random.normal(k, shape, jnp.float32)

    x = jax.random.normal(ks[0], (BATCH, SEQ, D_MODEL), jnp.float32)
    offset = jax.random.randint(ks[1], (BATCH,), 0, 64, dtype=jnp.int32) * CHUNK
    positions = offset[:, None] + jnp.arange(SEQ, dtype=jnp.int32)[None, :]
    H = MLA_HEADS
    return {
        "x": x,
        "positions": positions,
        "norm_mix": gain(ks[2], (DEPTH, D_MODEL)),
        "w_in": nrm(ks[3], (DEPTH, D_MODEL, IN_COLS), D_MODEL),
        "b_gate": 0.02 * jax.random.normal(ks[4], (DEPTH, N_BRANCH * D_MODEL), jnp.float32),
        "conv_w": nrm(ks[5], (DEPTH, CONV_K, 1, CONV_WIDTH), CONV_K),
        "w_conv_out": nrm(ks[6], (DEPTH, CONV_WIDTH, D_MODEL), CONV_WIDTH),
        "norm_q": gain(ks[7], (DEPTH, Q_LORA)),
        "norm_kv": gain(ks[8], (DEPTH, KV_LORA)),
        "w_uq": nrm(ks[9], (DEPTH, Q_LORA, H * (QK_NOPE + QK_ROPE)), Q_LORA),
        "w_ukv": nrm(ks[10], (DEPTH, KV_LORA, H * (QK_NOPE + V_DIM)), KV_LORA),
        "w_attn_out": nrm(ks[11], (DEPTH, H * V_DIM, D_MODEL), H * V_DIM),
        "w_mix_out": nrm(ks[12], (DEPTH, D_MODEL, D_MODEL), D_MODEL),
        "norm_ffn": gain(ks[13], (DEPTH, D_MODEL)),
        "dense_w_gate": nrm(ks[14], (N_DENSE, D_MODEL, D_FF_DENSE), D_MODEL),
        "dense_w_up": nrm(ks[15], (N_DENSE, D_MODEL, D_FF_DENSE), D_MODEL),
        "dense_w_down": nrm(ks[16], (N_DENSE, D_FF_DENSE, D_MODEL), D_FF_DENSE),
        "router_w": nrm(ks[17], (N_MOE, D_MODEL, N_EXPERTS), D_MODEL),
        "moe_w_gate": nrm(ks[18], (N_MOE, N_EXPERTS, D_MODEL, D_FF_EXPERT), D_MODEL),
        "moe_w_up": nrm(ks[19], (N_MOE, N_EXPERTS, D_MODEL, D_FF_EXPERT), D_MODEL),
        "moe_w_down": nrm(ks[20], (N_MOE, N_EXPERTS, D_FF_EXPERT, D_MODEL), D_FF_EXPERT),
        "norm_final": gain(ks[21], (D_MODEL,)),
    }


def reference(x, positions, norm_mix, w_in, b_gate, conv_w, w_conv_out, norm_q, norm_kv,
              w_uq, w_ukv, w_attn_out, w_mix_out, norm_ffn, dense_w_gate, dense_w_up,
              dense_w_down, router_w, moe_w_gate, moe_w_up, moe_w_down, norm_final):
    cos, sin = rope_tables(positions)
    split_at = [int(v) for v in np.cumsum(IN_SIZES)[:-1]]
    h = x
    for layer in range(DEPTH):
        xn = rmsnorm(h, norm_mix[layer])
        proj = xn @ w_in[layer]
        gate_b, gate_c, u, c_q, c_kv_full, gate_logits = jnp.split(proj, split_at, axis=-1)
        y_a = short_conv_mixer(gate_b, gate_c, u, conv_w[layer], w_conv_out[layer])
        y_b = mla_mixer(c_q, c_kv_full, cos, sin, norm_q[layer], norm_kv[layer],
                        w_uq[layer], w_ukv[layer], w_attn_out[layer])
        g_a, g_b = jnp.split(jax.nn.sigmoid(gate_logits + b_gate[layer]), N_BRANCH, axis=-1)
        h = h + (g_a * y_a + g_b * y_b) @ w_mix_out[layer]
        xn = rmsnorm(h, norm_ffn[layer])
        i = layer // 2
        if layer % 2 == 0:
            f = swiglu(xn, dense_w_gate[i], dense_w_up[i], dense_w_down[i])
        else:
            f = moe_swiglu(xn, router_w[i], moe_w_gate[i], moe_w_up[i], moe_w_down[i])
        h = h + f
    return rmsnorm(h, norm_final)
```

```python
import functools

import jax
import jax.numpy as jnp
from jax import lax
from jax.experimental import pallas as pl
from jax.experimental.pallas import tpu as pltpu

F32 = jnp.float32
BF16 = jnp.bfloat16

D_MODEL = 1024
DEPTH = 2
CHUNK = 64
CONV_WIDTH = 1024
CONV_K = 3
MLA_HEADS = 16
QK_NOPE = 64
QK_ROPE = 32
V_DIM = 64
Q_LORA = 256
KV_LORA = 128
ROPE_THETA = 10000.0
N_EXPERTS = 8
EPS = 1e-6

LANES = 128
SUBLANES = 8
HEAD_PAD = LANES
HALF_ROPE = QK_ROPE // 2

C_GATE_B = 0
C_GATE_C = C_GATE_B + CONV_WIDTH
C_U = C_GATE_C + CONV_WIDTH
C_Q = C_U + CONV_WIDTH
C_KV = C_Q + Q_LORA
C_KR = C_KV + KV_LORA
C_KRS = C_KR + LANES
C_GL = C_KRS + LANES
IN_COLS_P = C_GL + 2 * D_MODEL

TM_FRONT = 256
TM_MIX = 512
TM_FFN = 512
TQ = 256
VMEM_LIMIT = 48 * 1024 * 1024


def _rms(x, g):
    inv = lax.rsqrt(jnp.mean(x * x, axis=-1, keepdims=True) + EPS)
    return (x * inv) * g


def _dot(a, b):
    return jnp.dot(a, b, preferred_element_type=F32)


def _const_spec(shape):
    nd = len(shape)
    return pl.BlockSpec(shape, lambda *_: (0,) * nd)


def _rope_kernel(pos_ref, invf_ref, cos_ref, sin_ref, nsin_ref):
    ang = pos_ref[...].astype(F32) * invf_ref[...]
    c = jnp.cos(ang)
    s = jnp.sin(ang)
    cos_ref[...] = c
    sin_ref[...] = s
    nsin_ref[...] = -s


def _rope_tables(positions):
    t = positions.size
    rows = t * HALF_ROPE // LANES
    pos_rep = jnp.repeat(positions.reshape(t), HALF_ROPE).reshape(rows, LANES)
    inv_freq = ROPE_THETA ** (-jnp.arange(0, QK_ROPE, 2, dtype=F32) / QK_ROPE)
    invf = jnp.tile(inv_freq, LANES // HALF_ROPE).reshape(1, LANES)
    blk = 512
    spec = pl.BlockSpec((blk, LANES), lambda i: (i, 0))
    cos, sin, nsin = pl.pallas_call(
        _rope_kernel,
        out_shape=[jax.ShapeDtypeStruct((rows, LANES), F32)] * 3,
        grid=(rows // blk,),
        in_specs=[spec, _const_spec((1, LANES))],
        out_specs=[spec] * 3,
        name="rope_tables",
    )(pos_rep, invf)
    cos = cos.reshape(t, HALF_ROPE)
    sin = sin.reshape(t, HALF_ROPE)
    nsin = nsin.reshape(t, HALF_ROPE)
    ones = jnp.ones((t, QK_NOPE), F32)
    zeros = jnp.zeros((t, QK_NOPE), F32)
    pad = jnp.zeros((t, HEAD_PAD - QK_NOPE - QK_ROPE), F32)
    c_tab = jnp.concatenate([ones, cos, cos, pad], axis=1)
    s_tab = jnp.concatenate([zeros, nsin, sin, pad], axis=1)
    return c_tab, s_tab


def _front_kernel(tiles_per_seq, x_ref, gmix_ref, win_ref, bgate_ref, convw_ref, wco_ref,
                  gq_ref, gkv_ref, wuq_ref, wuqs_ref, wuk_ref, wuv_ref, c_ref, s_ref,
                  q_ref, k_ref, v_ref, gaya_ref, gb_ref, zbuf):
    tm = x_ref.shape[0]
    i = pl.program_id(0)
    xn = _rms(x_ref[...], gmix_ref[...]).astype(BF16)

    def proj(lo, width):
        return _dot(xn, win_ref[:, lo:lo + width])

    z = proj(C_GATE_C, CONV_WIDTH) * proj(C_U, CONV_WIDTH)

    @pl.when(i % tiles_per_seq == 0)
    def _():
        zbuf[0:SUBLANES, :] = jnp.zeros((SUBLANES, CONV_WIDTH), F32)

    zbuf[SUBLANES:SUBLANES + tm, :] = z
    cw = convw_ref[...]
    conv = (cw[2:3, :] * z
            + cw[1:2, :] * zbuf[SUBLANES - 1:SUBLANES - 1 + tm, :]
            + cw[0:1, :] * zbuf[SUBLANES - 2:SUBLANES - 2 + tm, :])
    zbuf[0:SUBLANES, :] = zbuf[tm:tm + SUBLANES, :]
    y_a = _dot((proj(C_GATE_B, CONV_WIDTH) * conv).astype(BF16), wco_ref[...])

    gate = jax.nn.sigmoid(proj(C_GL, 2 * D_MODEL) + bgate_ref[...])
    gaya_ref[...] = (gate[:, :D_MODEL] * y_a).astype(BF16)
    gb_ref[...] = gate[:, D_MODEL:].astype(BF16)

    c_tab = c_ref[...]
    s_tab = s_ref[...]
    c_all = jnp.tile(c_tab, (1, MLA_HEADS))
    s_all = jnp.tile(s_tab, (1, MLA_HEADS))
    cqn = _rms(proj(C_Q, Q_LORA), gq_ref[...]).astype(BF16)
    q = _dot(cqn, wuq_ref[...]) * c_all + _dot(cqn, wuqs_ref[...]) * s_all
    scale = (QK_NOPE + QK_ROPE) ** -0.5
    q_ref[...] = (q * scale).astype(BF16)

    ckvn = _rms(proj(C_KV, KV_LORA), gkv_ref[...]).astype(BF16)
    k_rot = proj(C_KR, LANES) * c_tab + proj(C_KRS, LANES) * s_tab
    k_ref[...] = (_dot(ckvn, wuk_ref[...]) + jnp.tile(k_rot, (1, MLA_HEADS))).astype(BF16)
    v_ref[...] = _dot(ckvn, wuv_ref[...]).astype(BF16)


def _front(h, seq, gmix, win, bgate, convw, wco, gq, gkv, wuq, wuqs, wuk, wuv, c_tab, s_tab):
    t = h.shape[0]
    tm = TM_FRONT
    hp = MLA_HEADS * HEAD_PAD
    row = lambda w: pl.BlockSpec((tm, w), lambda i: (i, 0))
    return pl.pallas_call(
        functools.partial(_front_kernel, seq // tm),
        out_shape=[jax.ShapeDtypeStruct((t, hp), BF16),
                   jax.ShapeDtypeStruct((t, hp), BF16),
                   jax.ShapeDtypeStruct((t, MLA_HEADS * V_DIM), BF16),
                   jax.ShapeDtypeStruct((t, D_MODEL), BF16),
                   jax.ShapeDtypeStruct((t, D_MODEL), BF16)],
        grid=(t // tm,),
        in_specs=[row(D_MODEL), _const_spec(gmix.shape), _const_spec(win.shape),
                  _const_spec(bgate.shape), _const_spec(convw.shape), _const_spec(wco.shape),
                  _const_spec(gq.shape), _const_spec(gkv.shape), _const_spec(wuq.shape),
                  _const_spec(wuqs.shape), _const_spec(wuk.shape), _const_spec(wuv.shape),
                  row(LANES), row(LANES)],
        out_specs=[row(hp), row(hp), row(MLA_HEADS * V_DIM), row(D_MODEL), row(D_MODEL)],
        scratch_shapes=[pltpu.VMEM((tm + SUBLANES, CONV_WIDTH), F32)],
        compiler_params=pltpu.CompilerParams(
            dimension_semantics=("arbitrary",), vmem_limit_bytes=VMEM_LIMIT),
        name="mixer_front",
    )(h, gmix, win, bgate, convw, wco, gq, gkv, wuq, wuqs, wuk, wuv, c_tab, s_tab)


def _attn_kernel(q_ref, k_ref, v_ref, o_ref):
    tq = q_ref.shape[0]
    qi = pl.program_id(2)
    row_chunk = lax.broadcasted_iota(jnp.int32, (tq, tq), 0) // CHUNK
    col_chunk = lax.broadcasted_iota(jnp.int32, (tq, tq), 1) // CHUNK
    diag_mask = col_chunk <= row_chunk

    def tile(j, carry, masked):
        off = pl.multiple_of(j * tq, tq)
        v2 = v_ref[pl.ds(off, tq), :]
        out = []
        for hh in range(2):
            m, l, acc = carry[hh]
            q = q_ref[:, hh * HEAD_PAD:(hh + 1) * HEAD_PAD]
            k = k_ref[pl.ds(off, tq), hh * HEAD_PAD:(hh + 1) * HEAD_PAD]
            s = lax.dot_general(q, k, (((1,), (1,)), ((), ())), preferred_element_type=F32)
            if masked:
                s = jnp.where(diag_mask, s, -jnp.inf)
            m_new = jnp.maximum(m, jnp.max(s, axis=-1, keepdims=True))
            alpha = jnp.exp(m - m_new)
            p = jnp.exp(s - m_new)
            l = alpha * l + jnp.sum(p, axis=-1, keepdims=True)
            acc = alpha * acc + _dot(p.astype(BF16), v2)
            out.append((m_new, l, acc))
        return tuple(out)

    init1 = (jnp.full((tq, 1), -jnp.inf, F32), jnp.zeros((tq, 1), F32),
             jnp.zeros((tq, 2 * V_DIM), F32))
    carry = lax.fori_loop(0, qi, lambda j, c: tile(j, c, False), (init1, init1))
    (_, l0, acc0), (_, l1, acc1) = tile(qi, carry, True)
    lane = lax.broadcasted_iota(jnp.int32, (tq, 2 * V_DIM), 1)
    o_ref[...] = jnp.where(lane < V_DIM, acc0 / l0, acc1 / l1).astype(o_ref.dtype)


def _attention(q, k, v, batch, seq):
    t = q.shape[0]
    nq = seq // TQ
    pairs = MLA_HEADS // 2
    return pl.pallas_call(
        _attn_kernel,
        out_shape=jax.ShapeDtypeStruct((t, MLA_HEADS * V_DIM), BF16),
        grid=(batch, pairs, nq),
        in_specs=[pl.BlockSpec((TQ, 2 * HEAD_PAD), lambda b, p, i: (b * nq + i, p)),
                  pl.BlockSpec((seq, 2 * HEAD_PAD), lambda b, p, i: (b, p)),
                  pl.BlockSpec((seq, 2 * V_DIM), lambda b, p, i: (b, p))],
        out_specs=pl.BlockSpec((TQ, 2 * V_DIM), lambda b, p, i: (b * nq + i, p)),
        compiler_params=pltpu.CompilerParams(
            dimension_semantics=("parallel", "parallel", "arbitrary"),
            vmem_limit_bytes=VMEM_LIMIT),
        name="chunk_attention",
    )(q, k, v)


def _merge_body(o_ref, gaya_ref, gb_ref, h_ref, wo_ref, wmix_ref, gffn_ref, hmid_ref, xn_ref):
    y_b = _dot(o_ref[...], wo_ref[...])
    merged = gaya_ref[...].astype(F32) + gb_ref[...].astype(F32) * y_b
    h_mid = h_ref[...] + _dot(merged.astype(BF16), wmix_ref[...])
    hmid_ref[...] = h_mid
    xn = _rms(h_mid, gffn_ref[...])
    xn_ref[...] = xn.astype(BF16)
    return xn


def _merge_kernel(o_ref, gaya_ref, gb_ref, h_ref, wo_ref, wmix_ref, gffn_ref, hmid_ref, xn_ref):
    _merge_body(o_ref, gaya_ref, gb_ref, h_ref, wo_ref, wmix_ref, gffn_ref, hmid_ref, xn_ref)


def _merge_router_kernel(o_ref, gaya_ref, gb_ref, h_ref, wo_ref, wmix_ref, gffn_ref,
                         wrh_ref, wrl_ref, hmid_ref, xn_ref, gates_ref):
    xn = _merge_body(o_ref, gaya_ref, gb_ref, h_ref, wo_ref, wmix_ref, gffn_ref, hmid_ref, xn_ref)
    x_hi = xn.astype(BF16)
    x_lo = (xn - x_hi.astype(F32)).astype(BF16)
    logits = _dot(x_hi, wrh_ref[...]) + (_dot(x_hi, wrl_ref[...]) + _dot(x_lo, wrh_ref[...]))
    lane = lax.broadcasted_iota(jnp.int32, logits.shape, 1)
    lg = jnp.where(lane < N_EXPERTS, logits, -jnp.inf)
    m1 = jnp.max(lg, axis=-1, keepdims=True)
    i1 = jnp.min(jnp.where(lg == m1, lane, LANES), axis=-1, keepdims=True)
    lg2 = jnp.where(lane == i1, -jnp.inf, lg)
    m2 = jnp.max(lg2, axis=-1, keepdims=True)
    i2 = jnp.min(jnp.where(lg2 == m2, lane, LANES), axis=-1, keepdims=True)
    e2 = jnp.exp(m2 - m1)
    den = 1.0 + e2
    gates_ref[...] = jnp.where(lane == i1, 1.0 / den, 0.0) + jnp.where(lane == i2, e2 / den, 0.0)


def _merge(o, gaya, gb, h, wo, wmix, gffn, router=None):
    t = h.shape[0]
    tm = TM_MIX
    row = lambda w: pl.BlockSpec((tm, w), lambda i: (i, 0))
    in_specs = [row(D_MODEL)] * 4 + [_const_spec(wo.shape), _const_spec(wmix.shape),
                                     _const_spec(gffn.shape)]
    out_shape = [jax.ShapeDtypeStruct((t, D_MODEL), F32), jax.ShapeDtypeStruct((t, D_MODEL), BF16)]
    out_specs = [row(D_MODEL), row(D_MODEL)]
    args = [o, gaya, gb, h, wo, wmix, gffn]
    body = _merge_kernel
    if router is not None:
        body = _merge_router_kernel
        in_specs += [_const_spec(router[0].shape), _const_spec(router[1].shape)]
        out_shape.append(jax.ShapeDtypeStruct((t, LANES), F32))
        out_specs.append(row(LANES))
        args += list(router)
    return pl.pallas_call(
        body, out_shape=out_shape, grid=(t // tm,), in_specs=in_specs, out_specs=out_specs,
        compiler_params=pltpu.CompilerParams(
            dimension_semantics=("parallel",), vmem_limit_bytes=VMEM_LIMIT),
        name="merge_router" if router is not None else "merge",
    )(*args)


def _swiglu(x, wg, wu, wd):
    g = _dot(x, wg)
    u = _dot(x, wu)
    return _dot((g * jax.nn.sigmoid(g) * u).astype(BF16), wd)


def _dense_ffn_kernel(x_ref, hmid_ref, wg_ref, wu_ref, wd_ref, out_ref):
    f = pl.program_id(1)

    @pl.when(f == 0)
    def _():
        out_ref[...] = hmid_ref[...]

    out_ref[...] += _swiglu(x_ref[...], wg_ref[...], wu_ref[...], wd_ref[...])


def _dense_ffn(xn, h_mid, wg, wu, wd):
    t = xn.shape[0]
    tm = TM_FFN
    d_ff = wg.shape[1]
    nf = 2
    tf = d_ff // nf
    row = pl.BlockSpec((tm, D_MODEL), lambda i, f: (i, 0))
    return pl.pallas_call(
        _dense_ffn_kernel,
        out_shape=jax.ShapeDtypeStruct((t, D_MODEL), F32),
        grid=(t // tm, nf),
        in_specs=[row, row,
                  pl.BlockSpec((D_MODEL, tf), lambda i, f: (0, f)),
                  pl.BlockSpec((D_MODEL, tf), lambda i, f: (0, f)),
                  pl.BlockSpec((tf, D_MODEL), lambda i, f: (f, 0))],
        out_specs=row,
        compiler_params=pltpu.CompilerParams(
            dimension_semantics=("parallel", "arbitrary"), vmem_limit_bytes=VMEM_LIMIT),
        name="dense_ffn",
    )(xn, h_mid, wg, wu, wd)


def _moe_ffn_kernel(final_norm, x_ref, gates_ref, hmid_ref, wg_ref, wu_ref, wd_ref, gfin_ref,
                    out_ref, acc_ref):
    e = pl.program_id(1)

    @pl.when(e == 0)
    def _():
        acc_ref[...] = hmid_ref[...]

    gates = gates_ref[...]
    lane = lax.broadcasted_iota(jnp.int32, gates.shape, 1)
    gate = jnp.sum(jnp.where(lane == e, gates, 0.0), axis=-1, keepdims=True)
    acc_ref[...] += gate * _swiglu(x_ref[...], wg_ref[...], wu_ref[...], wd_ref[...])

    @pl.when(e == pl.num_programs(1) - 1)
    def _():
        h = acc_ref[...]
        out_ref[...] = _rms(h, gfin_ref[...]) if final_norm else h


def _moe_ffn(xn, gates, h_mid, wg, wu, wd, gfin, final_norm):
    t = xn.shape[0]
    tm = TM_FFN
    n_e, _, d_ff = wg.shape
    row = lambda w: pl.BlockSpec((tm, w), lambda i, e: (i, 0))
    return pl.pallas_call(
        functools.partial(_moe_ffn_kernel, final_norm),
        out_shape=jax.ShapeDtypeStruct((t, D_MODEL), F32),
        grid=(t // tm, n_e),
        in_specs=[row(D_MODEL), row(LANES), row(D_MODEL),
                  pl.BlockSpec((None, D_MODEL, d_ff), lambda i, e: (e, 0, 0)),
                  pl.BlockSpec((None, D_MODEL, d_ff), lambda i, e: (e, 0, 0)),
                  pl.BlockSpec((None, d_ff, D_MODEL), lambda i, e: (e, 0, 0)),
                  _const_spec(gfin.shape)],
        out_specs=row(D_MODEL),
        scratch_shapes=[pltpu.VMEM((tm, D_MODEL), F32)],
        compiler_params=pltpu.CompilerParams(
            dimension_semantics=("parallel", "arbitrary"), vmem_limit_bytes=VMEM_LIMIT),
        name="moe_ffn",
    )(xn, gates, h_mid, wg, wu, wd, gfin)


def _final_norm_kernel(h_ref, g_ref, out_ref):
    out_ref[...] = _rms(h_ref[...], g_ref[...])


def _final_norm(h, g):
    t = h.shape[0]
    row = pl.BlockSpec((TM_FFN, D_MODEL), lambda i: (i, 0))
    return pl.pallas_call(
        _final_norm_kernel, out_shape=jax.ShapeDtypeStruct((t, D_MODEL), F32),
        grid=(t // TM_FFN,), in_specs=[row, _const_spec(g.shape)], out_specs=row,
        name="final_norm",
    )(h, g)


def _pack_w_in(w):
    d = w.shape[0]
    kr0 = 3 * CONV_WIDTH + Q_LORA + KV_LORA
    kr_a = w[:, kr0:kr0 + HALF_ROPE]
    kr_b = w[:, kr0 + HALF_ROPE:kr0 + QK_ROPE]
    z_lo = jnp.zeros((d, QK_NOPE), w.dtype)
    z_hi = jnp.zeros((d, HEAD_PAD - QK_NOPE - QK_ROPE), w.dtype)
    return jnp.concatenate(
        [w[:, :kr0], z_lo, kr_a, kr_b, z_hi, z_lo, kr_b, kr_a, z_hi, w[:, kr0 + QK_ROPE:]], axis=1)


def _pack_w_uq(w):
    w3 = w.reshape(Q_LORA, MLA_HEADS, QK_NOPE + QK_ROPE)
    nope = w3[..., :QK_NOPE]
    r_a = w3[..., QK_NOPE:QK_NOPE + HALF_ROPE]
    r_b = w3[..., QK_NOPE + HALF_ROPE:]
    pad = jnp.zeros((Q_LORA, MLA_HEADS, HEAD_PAD - QK_NOPE - QK_ROPE), w.dtype)
    plain = jnp.concatenate([nope, r_a, r_b, pad], axis=-1)
    swapped = jnp.concatenate([jnp.zeros_like(nope), r_b, r_a, pad], axis=-1)
    return (plain.reshape(Q_LORA, MLA_HEADS * HEAD_PAD),
            swapped.reshape(Q_LORA, MLA_HEADS * HEAD_PAD))


def _pack_w_ukv(w):
    w3 = w.reshape(KV_LORA, MLA_HEADS, QK_NOPE + V_DIM)
    k = jnp.concatenate([w3[..., :QK_NOPE],
                         jnp.zeros((KV_LORA, MLA_HEADS, HEAD_PAD - QK_NOPE), w.dtype)], axis=-1)
    return (k.reshape(KV_LORA, MLA_HEADS * HEAD_PAD),
            w3[..., QK_NOPE:].reshape(KV_LORA, MLA_HEADS * V_DIM))


def kernel(x, positions, norm_mix, w_in, b_gate, conv_w, w_conv_out, norm_q, norm_kv, w_uq, w_ukv,
           w_attn_out, w_mix_out, norm_ffn, dense_w_gate, dense_w_up, dense_w_down, router_w,
           moe_w_gate, moe_w_up, moe_w_down, norm_final):
    batch, seq, d = x.shape
    t = batch * seq
    c_tab, s_tab = _rope_tables(positions)
    h = x.reshape(t, d)
    for layer in range(DEPTH):
        wuq, wuqs = _pack_w_uq(w_uq[layer])
        wuk, wuv = _pack_w_ukv(w_ukv[layer])
        q, k, v, gaya, gb = _front(
            h, seq, norm_mix[layer].reshape(1, d), _pack_w_in(w_in[layer]).astype(BF16),
            b_gate[layer].reshape(1, 2 * d), conv_w[layer].reshape(CONV_K, CONV_WIDTH),
            w_conv_out[layer].astype(BF16), norm_q[layer].reshape(1, Q_LORA),
            norm_kv[layer].reshape(1, KV_LORA), wuq.astype(BF16), wuqs.astype(BF16),
            wuk.astype(BF16), wuv.astype(BF16), c_tab, s_tab)
        o = _attention(q, k, v, batch, seq)
        i = layer // 2
        last = layer == DEPTH - 1
        merge_args = (o, gaya, gb, h, w_attn_out[layer].astype(BF16),
                      w_mix_out[layer].astype(BF16), norm_ffn[layer].reshape(1, d))
        if layer % 2 == 0:
            h_mid, xn = _merge(*merge_args)
            h = _dense_ffn(xn, h_mid, dense_w_gate[i].astype(BF16), dense_w_up[i].astype(BF16),
                           dense_w_down[i].astype(BF16))
            if last:
                h = _final_norm(h, norm_final.reshape(1, d))
        else:
            wr = jnp.pad(router_w[i], ((0, 0), (0, LANES - N_EXPERTS)))
            wr_hi = wr.astype(BF16)
            wr_lo = (wr - wr_hi.astype(F32)).astype(BF16)
            h_mid, xn, gates = _merge(*merge_args, router=(wr_hi, wr_lo))
            h = _moe_ffn(xn, gates, h_mid, moe_w_gate[i].astype(BF16), moe_w_up[i].astype(BF16),
                         moe_w_down[i].astype(BF16), norm_final.reshape(1, d), last)
    return h.reshape(batch, seq, d)
```

```python
import functools

import jax
import jax.numpy as jnp
from jax import lax
from jax.experimental import pallas as pl
from jax.experimental.pallas import tpu as pltpu

F32 = jnp.float32
BF16 = jnp.bfloat16

D_MODEL = 1024
DEPTH = 2
CHUNK = 64
CONV_WIDTH = 1024
CONV_K = 3
MLA_HEADS = 16
QK_NOPE = 64
QK_ROPE = 32
V_DIM = 64
Q_LORA = 256
KV_LORA = 128
ROPE_THETA = 10000.0
N_EXPERTS = 8
EPS = 1e-6
LOG2_E = 1.4426950408889634

LANES = 128
SUBLANES = 8
HEAD_PAD = LANES
HALF_ROPE = QK_ROPE // 2

C_GATE_B = 0
C_GATE_C = C_GATE_B + CONV_WIDTH
C_U = C_GATE_C + CONV_WIDTH
C_Q = C_U + CONV_WIDTH
C_KV = C_Q + Q_LORA
C_KR = C_KV + KV_LORA
C_KRS = C_KR + LANES
C_GL = C_KRS + LANES
IN_COLS_P = C_GL + 2 * D_MODEL

TM_FRONT = 256
TM_MIX = 512
TM_FFN = 512
TQ = 256
VMEM_LIMIT = 48 * 1024 * 1024


def _rms(x, g):
    inv = lax.rsqrt(jnp.mean(x * x, axis=-1, keepdims=True) + EPS)
    return (x * inv) * g


def _dot(a, b):
    return jnp.dot(a, b, preferred_element_type=F32)


def _const_spec(shape):
    nd = len(shape)
    return pl.BlockSpec(shape, lambda *_: (0,) * nd)


def _rope_kernel(pos_ref, invf_ref, cos_ref, sin_ref, nsin_ref):
    ang = pos_ref[...].astype(F32) * invf_ref[...]
    c = jnp.cos(ang)
    s = jnp.sin(ang)
    cos_ref[...] = c
    sin_ref[...] = s
    nsin_ref[...] = -s


def _rope_tables(positions):
    t = positions.size
    rows = t * HALF_ROPE // LANES
    pos_rep = jnp.repeat(positions.reshape(t), HALF_ROPE).reshape(rows, LANES)
    inv_freq = ROPE_THETA ** (-jnp.arange(0, QK_ROPE, 2, dtype=F32) / QK_ROPE)
    invf = jnp.tile(inv_freq, LANES // HALF_ROPE).reshape(1, LANES)
    blk = 512
    spec = pl.BlockSpec((blk, LANES), lambda i: (i, 0))
    cos, sin, nsin = pl.pallas_call(
        _rope_kernel,
        out_shape=[jax.ShapeDtypeStruct((rows, LANES), F32)] * 3,
        grid=(rows // blk,),
        in_specs=[spec, _const_spec((1, LANES))],
        out_specs=[spec] * 3,
        name="rope_tables",
    )(pos_rep, invf)
    cos = cos.reshape(t, HALF_ROPE)
    sin = sin.reshape(t, HALF_ROPE)
    nsin = nsin.reshape(t, HALF_ROPE)
    ones = jnp.ones((t, QK_NOPE), F32)
    zeros = jnp.zeros((t, QK_NOPE), F32)
    pad = jnp.zeros((t, HEAD_PAD - QK_NOPE - QK_ROPE), F32)
    c_tab = jnp.concatenate([ones, cos, cos, pad], axis=1)
    s_tab = jnp.concatenate([zeros, nsin, sin, pad], axis=1)
    return c_tab, s_tab


def _front_kernel(tiles_per_seq, x_ref, gmix_ref, win_ref, bgate_ref, convw_ref, wco_ref,
                  gq_ref, gkv_ref, wuq_ref, wuqs_ref, wuk_ref, wuv_ref, c_ref, s_ref,
                  q_ref, k_ref, v_ref, gaya_ref, gb_ref, zbuf):
    tm = x_ref.shape[0]
    i = pl.program_id(0)
    xn = _rms(x_ref[...], gmix_ref[...]).astype(BF16)

    def proj(lo, width):
        return _dot(xn, win_ref[:, lo:lo + width])

    z = proj(C_GATE_C, CONV_WIDTH) * proj(C_U, CONV_WIDTH)

    @pl.when(i % tiles_per_seq == 0)
    def _():
        zbuf[0:SUBLANES, :] = jnp.zeros((SUBLANES, CONV_WIDTH), F32)

    zbuf[SUBLANES:SUBLANES + tm, :] = z
    cw = convw_ref[...]
    conv = (cw[2:3, :] * z
            + cw[1:2, :] * zbuf[SUBLANES - 1:SUBLANES - 1 + tm, :]
            + cw[0:1, :] * zbuf[SUBLANES - 2:SUBLANES - 2 + tm, :])
    zbuf[0:SUBLANES, :] = zbuf[tm:tm + SUBLANES, :]
    y_a = _dot((proj(C_GATE_B, CONV_WIDTH) * conv).astype(BF16), wco_ref[...])

    gate = jax.nn.sigmoid(proj(C_GL, 2 * D_MODEL) + bgate_ref[...])
    gaya_ref[...] = (gate[:, :D_MODEL] * y_a).astype(BF16)
    gb_ref[...] = gate[:, D_MODEL:].astype(BF16)

    c_tab = c_ref[...]
    s_tab = s_ref[...]
    c_all = jnp.tile(c_tab, (1, MLA_HEADS))
    s_all = jnp.tile(s_tab, (1, MLA_HEADS))
    cqn = _rms(proj(C_Q, Q_LORA), gq_ref[...]).astype(BF16)
    q = _dot(cqn, wuq_ref[...]) * c_all + _dot(cqn, wuqs_ref[...]) * s_all
    q_ref[...] = (q * (LOG2_E * (QK_NOPE + QK_ROPE) ** -0.5)).astype(BF16)

    ckvn = _rms(proj(C_KV, KV_LORA), gkv_ref[...]).astype(BF16)
    k_rot = proj(C_KR, LANES) * c_tab + proj(C_KRS, LANES) * s_tab
    k_ref[...] = (_dot(ckvn, wuk_ref[...]) + jnp.tile(k_rot, (1, MLA_HEADS))).astype(BF16)
    v_ref[...] = _dot(ckvn, wuv_ref[...]).astype(BF16)


def _front(h, seq, gmix, win, bgate, convw, wco, gq, gkv, wuq, wuqs, wuk, wuv, c_tab, s_tab):
    t = h.shape[0]
    tm = TM_FRONT
    hp = MLA_HEADS * HEAD_PAD
    row = lambda w: pl.BlockSpec((tm, w), lambda i: (i, 0))
    return pl.pallas_call(
        functools.partial(_front_kernel, seq // tm),
        out_shape=[jax.ShapeDtypeStruct((t, hp), BF16),
                   jax.ShapeDtypeStruct((t, hp), BF16),
                   jax.ShapeDtypeStruct((t, MLA_HEADS * V_DIM), BF16),
                   jax.ShapeDtypeStruct((t, D_MODEL), BF16),
                   jax.ShapeDtypeStruct((t, D_MODEL), BF16)],
        grid=(t // tm,),
        in_specs=[row(D_MODEL), _const_spec(gmix.shape), _const_spec(win.shape),
                  _const_spec(bgate.shape), _const_spec(convw.shape), _const_spec(wco.shape),
                  _const_spec(gq.shape), _const_spec(gkv.shape), _const_spec(wuq.shape),
                  _const_spec(wuqs.shape), _const_spec(wuk.shape), _const_spec(wuv.shape),
                  row(LANES), row(LANES)],
        out_specs=[row(hp), row(hp), row(MLA_HEADS * V_DIM), row(D_MODEL), row(D_MODEL)],
        scratch_shapes=[pltpu.VMEM((tm + SUBLANES, CONV_WIDTH), F32)],
        compiler_params=pltpu.CompilerParams(
            dimension_semantics=("arbitrary",), vmem_limit_bytes=VMEM_LIMIT),
        name="mixer_front",
    )(h, gmix, win, bgate, convw, wco, gq, gkv, wuq, wuqs, wuk, wuv, c_tab, s_tab)


def _attn_kernel(tq, q_ref, k_ref, v_ref, o_ref):
    seq = q_ref.shape[0]
    row_chunk = lax.broadcasted_iota(jnp.int32, (tq, tq), 0) // CHUNK
    col_chunk = lax.broadcasted_iota(jnp.int32, (tq, tq), 1) // CHUNK
    diag_mask = col_chunk <= row_chunk
    lane = lax.broadcasted_iota(jnp.int32, (tq, 2 * V_DIM), 1)
    contract_last = (((1,), (1,)), ((), ()))

    for qi in range(seq // tq):
        q_lo = qi * tq
        outs = []
        for hh in range(2):
            cols = slice(hh * HEAD_PAD, (hh + 1) * HEAD_PAD)
            q = q_ref[q_lo:q_lo + tq, cols]
            s_diag = lax.dot_general(q, k_ref[q_lo:q_lo + tq, cols], contract_last,
                                     preferred_element_type=F32)
            s_diag = jnp.where(diag_mask, s_diag, -jnp.inf)
            m = jnp.max(s_diag, axis=-1, keepdims=True)
            if qi:
                s_past = lax.dot_general(q, k_ref[0:q_lo, cols], contract_last,
                                         preferred_element_type=F32)
                m = jnp.maximum(m, jnp.max(s_past, axis=-1, keepdims=True))
            p_diag = jnp.exp2(s_diag - m)
            l = jnp.sum(p_diag, axis=-1, keepdims=True)
            acc = _dot(p_diag.astype(BF16), v_ref[q_lo:q_lo + tq, :])
            if qi:
                p_past = jnp.exp2(s_past - m)
                l = l + jnp.sum(p_past, axis=-1, keepdims=True)
                acc = acc + _dot(p_past.astype(BF16), v_ref[0:q_lo, :])
            outs.append(acc / l)
        o_ref[q_lo:q_lo + tq, :] = jnp.where(lane < V_DIM, outs[0], outs[1]).astype(o_ref.dtype)


def _attention(q, k, v, batch, seq):
    t = q.shape[0]
    pairs = MLA_HEADS // 2
    return pl.pallas_call(
        functools.partial(_attn_kernel, TQ),
        out_shape=jax.ShapeDtypeStruct((t, MLA_HEADS * V_DIM), BF16),
        grid=(batch, pairs),
        in_specs=[pl.BlockSpec((seq, 2 * HEAD_PAD), lambda b, p: (b, p)),
                  pl.BlockSpec((seq, 2 * HEAD_PAD), lambda b, p: (b, p)),
                  pl.BlockSpec((seq, 2 * V_DIM), lambda b, p: (b, p))],
        out_specs=pl.BlockSpec((seq, 2 * V_DIM), lambda b, p: (b, p)),
        compiler_params=pltpu.CompilerParams(
            dimension_semantics=("parallel", "parallel"), vmem_limit_bytes=VMEM_LIMIT),
        name="chunk_attention",
    )(q, k, v)


def _merge_body(o_ref, gaya_ref, gb_ref, h_ref, wo_ref, wmix_ref, gffn_ref, hmid_ref, xn_ref):
    y_b = _dot(o_ref[...], wo_ref[...])
    merged = gaya_ref[...].astype(F32) + gb_ref[...].astype(F32) * y_b
    h_mid = h_ref[...] + _dot(merged.astype(BF16), wmix_ref[...])
    hmid_ref[...] = h_mid
    xn = _rms(h_mid, gffn_ref[...])
    xn_ref[...] = xn.astype(BF16)
    return xn


def _merge_kernel(o_ref, gaya_ref, gb_ref, h_ref, wo_ref, wmix_ref, gffn_ref, hmid_ref, xn_ref):
    _merge_body(o_ref, gaya_ref, gb_ref, h_ref, wo_ref, wmix_ref, gffn_ref, hmid_ref, xn_ref)


def _merge_router_kernel(o_ref, gaya_ref, gb_ref, h_ref, wo_ref, wmix_ref, gffn_ref,
                         wrh_ref, wrl_ref, hmid_ref, xn_ref, gates_ref):
    xn = _merge_body(o_ref, gaya_ref, gb_ref, h_ref, wo_ref, wmix_ref, gffn_ref, hmid_ref, xn_ref)
    x_hi = xn.astype(BF16)
    x_lo = (xn - x_hi.astype(F32)).astype(BF16)
    logits = _dot(x_hi, wrh_ref[...]) + (_dot(x_hi, wrl_ref[...]) + _dot(x_lo, wrh_ref[...]))
    lane = lax.broadcasted_iota(jnp.int32, logits.shape, 1)
    lg = jnp.where(lane < N_EXPERTS, logits, -jnp.inf)
    m1 = jnp.max(lg, axis=-1, keepdims=True)
    i1 = jnp.min(jnp.where(lg == m1, lane, LANES), axis=-1, keepdims=True)
    lg2 = jnp.where(lane == i1, -jnp.inf, lg)
    m2 = jnp.max(lg2, axis=-1, keepdims=True)
    i2 = jnp.min(jnp.where(lg2 == m2, lane, LANES), axis=-1, keepdims=True)
    e2 = jnp.exp(m2 - m1)
    den = 1.0 + e2
    gates_ref[...] = jnp.where(lane == i1, 1.0 / den, 0.0) + jnp.where(lane == i2, e2 / den, 0.0)


def _merge(o, gaya, gb, h, wo, wmix, gffn, router=None):
    t = h.shape[0]
    tm = TM_MIX
    row = lambda w: pl.BlockSpec((tm, w), lambda i: (i, 0))
    in_specs = [row(D_MODEL)] * 4 + [_const_spec(wo.shape), _const_spec(wmix.shape),
                                     _const_spec(gffn.shape)]
    out_shape = [jax.ShapeDtypeStruct((t, D_MODEL), F32), jax.ShapeDtypeStruct((t, D_MODEL), BF16)]
    out_specs = [row(D_MODEL), row(D_MODEL)]
    args = [o, gaya, gb, h, wo, wmix, gffn]
    body = _merge_kernel
    if router is not None:
        body = _merge_router_kernel
        in_specs += [_const_spec(router[0].shape), _const_spec(router[1].shape)]
        out_shape.append(jax.ShapeDtypeStruct((t, LANES), F32))
        out_specs.append(row(LANES))
        args += list(router)
    return pl.pallas_call(
        body, out_shape=out_shape, grid=(t // tm,), in_specs=in_specs, out_specs=out_specs,
        compiler_params=pltpu.CompilerParams(
            dimension_semantics=("parallel",), vmem_limit_bytes=VMEM_LIMIT),
        name="merge_router" if router is not None else "merge",
    )(*args)


def _swiglu(x, wg, wu, wd):
    g = _dot(x, wg)
    u = _dot(x, wu)
    return _dot((g * jax.nn.sigmoid(g) * u).astype(BF16), wd)


def _dense_ffn_kernel(x_ref, hmid_ref, wg_ref, wu_ref, wd_ref, out_ref):
    f = pl.program_id(1)

    @pl.when(f == 0)
    def _():
        out_ref[...] = hmid_ref[...]

    out_ref[...] += _swiglu(x_ref[...], wg_ref[...], wu_ref[...], wd_ref[...])


def _dense_ffn(xn, h_mid, wg, wu, wd):
    t = xn.shape[0]
    tm = TM_FFN
    d_ff = wg.shape[1]
    nf = 2
    tf = d_ff // nf
    row = pl.BlockSpec((tm, D_MODEL), lambda i, f: (i, 0))
    return pl.pallas_call(
        _dense_ffn_kernel,
        out_shape=jax.ShapeDtypeStruct((t, D_MODEL), F32),
        grid=(t // tm, nf),
        in_specs=[row, row,
                  pl.BlockSpec((D_MODEL, tf), lambda i, f: (0, f)),
                  pl.BlockSpec((D_MODEL, tf), lambda i, f: (0, f)),
                  pl.BlockSpec((tf, D_MODEL), lambda i, f: (f, 0))],
        out_specs=row,
        compiler_params=pltpu.CompilerParams(
            dimension_semantics=("parallel", "arbitrary"), vmem_limit_bytes=VMEM_LIMIT),
        name="dense_ffn",
    )(xn, h_mid, wg, wu, wd)


def _moe_ffn_kernel(final_norm, x_ref, gates_ref, hmid_ref, wg_ref, wu_ref, wd_ref, gfin_ref,
                    out_ref, acc_ref):
    e = pl.program_id(1)

    @pl.when(e == 0)
    def _():
        acc_ref[...] = hmid_ref[...]

    gates = gates_ref[...]
    lane = lax.broadcasted_iota(jnp.int32, gates.shape, 1)
    gate = jnp.sum(jnp.where(lane == e, gates, 0.0), axis=-1, keepdims=True)
    acc_ref[...] += gate * _swiglu(x_ref[...], wg_ref[...], wu_ref[...], wd_ref[...])

    @pl.when(e == pl.num_programs(1) - 1)
    def _():
        h = acc_ref[...]
        out_ref[...] = _rms(h, gfin_ref[...]) if final_norm else h


def _moe_ffn(xn, gates, h_mid, wg, wu, wd, gfin, final_norm):
    t = xn.shape[0]
    tm = TM_FFN
    n_e, _, d_ff = wg.shape
    row = lambda w: pl.BlockSpec((tm, w), lambda i, e: (i, 0))
    return pl.pallas_call(
        functools.partial(_moe_ffn_kernel, final_norm),
        out_shape=jax.ShapeDtypeStruct((t, D_MODEL), F32),
        grid=(t // tm, n_e),
        in_specs=[row(D_MODEL), row(LANES), row(D_MODEL),
                  pl.BlockSpec((None, D_MODEL, d_ff), lambda i, e: (e, 0, 0)),
                  pl.BlockSpec((None, D_MODEL, d_ff), lambda i, e: (e, 0, 0)),
                  pl.BlockSpec((None, d_ff, D_MODEL), lambda i, e: (e, 0, 0)),
                  _const_spec(gfin.shape)],
        out_specs=row(D_MODEL),
        scratch_shapes=[pltpu.VMEM((tm, D_MODEL), F32)],
        compiler_params=pltpu.CompilerParams(
            dimension_semantics=("parallel", "arbitrary"), vmem_limit_bytes=VMEM_LIMIT),
        name="moe_ffn",
    )(xn, gates, h_mid, wg, wu, wd, gfin)


def _final_norm_kernel(h_ref, g_ref, out_ref):
    out_ref[...] = _rms(h_ref[...], g_ref[...])


def _final_norm(h, g):
    t = h.shape[0]
    row = pl.BlockSpec((TM_FFN, D_MODEL), lambda i: (i, 0))
    return pl.pallas_call(
        _final_norm_kernel, out_shape=jax.ShapeDtypeStruct((t, D_MODEL), F32),
        grid=(t // TM_FFN,), in_specs=[row, _const_spec(g.shape)], out_specs=row,
        name="final_norm",
    )(h, g)


def _pack_w_in(w):
    d = w.shape[0]
    kr0 = 3 * CONV_WIDTH + Q_LORA + KV_LORA
    kr_a = w[:, kr0:kr0 + HALF_ROPE]
    kr_b = w[:, kr0 + HALF_ROPE:kr0 + QK_ROPE]
    z_lo = jnp.zeros((d, QK_NOPE), w.dtype)
    z_hi = jnp.zeros((d, HEAD_PAD - QK_NOPE - QK_ROPE), w.dtype)
    return jnp.concatenate(
        [w[:, :kr0], z_lo, kr_a, kr_b, z_hi, z_lo, kr_b, kr_a, z_hi, w[:, kr0 + QK_ROPE:]], axis=1)


def _pack_w_uq(w):
    w3 = w.reshape(Q_LORA, MLA_HEADS, QK_NOPE + QK_ROPE)
    nope = w3[..., :QK_NOPE]
    r_a = w3[..., QK_NOPE:QK_NOPE + HALF_ROPE]
    r_b = w3[..., QK_NOPE + HALF_ROPE:]
    pad = jnp.zeros((Q_LORA, MLA_HEADS, HEAD_PAD - QK_NOPE - QK_ROPE), w.dtype)
    plain = jnp.concatenate([nope, r_a, r_b, pad], axis=-1)
    swapped = jnp.concatenate([jnp.zeros_like(nope), r_b, r_a, pad], axis=-1)
    return (plain.reshape(Q_LORA, MLA_HEADS * HEAD_PAD),
            swapped.reshape(Q_LORA, MLA_HEADS * HEAD_PAD))


def _pack_w_ukv(w):
    w3 = w.reshape(KV_LORA, MLA_HEADS, QK_NOPE + V_DIM)
    k = jnp.concatenate([w3[..., :QK_NOPE],
                         jnp.zeros((KV_LORA, MLA_HEADS, HEAD_PAD - QK_NOPE), w.dtype)], axis=-1)
    return (k.reshape(KV_LORA, MLA_HEADS * HEAD_PAD),
            w3[..., QK_NOPE:].reshape(KV_LORA, MLA_HEADS * V_DIM))


def kernel(x, positions, norm_mix, w_in, b_gate, conv_w, w_conv_out, norm_q, norm_kv, w_uq, w_ukv,
           w_attn_out, w_mix_out, norm_ffn, dense_w_gate, dense_w_up, dense_w_down, router_w,
           moe_w_gate, moe_w_up, moe_w_down, norm_final):
    batch, seq, d = x.shape
    t = batch * seq
    c_tab, s_tab = _rope_tables(positions)
    h = x.reshape(t, d)
    for layer in range(DEPTH):
        wuq, wuqs = _pack_w_uq(w_uq[layer])
        wuk, wuv = _pack_w_ukv(w_ukv[layer])
        q, k, v, gaya, gb = _front(
            h, seq, norm_mix[layer].reshape(1, d), _pack_w_in(w_in[layer]).astype(BF16),
            b_gate[layer].reshape(1, 2 * d), conv_w[layer].reshape(CONV_K, CONV_WIDTH),
            w_conv_out[layer].astype(BF16), norm_q[layer].reshape(1, Q_LORA),
            norm_kv[layer].reshape(1, KV_LORA), wuq.astype(BF16), wuqs.astype(BF16),
            wuk.astype(BF16), wuv.astype(BF16), c_tab, s_tab)
        o = _attention(q, k, v, batch, seq)
        i = layer // 2
        last = layer == DEPTH - 1
        merge_args = (o, gaya, gb, h, w_attn_out[layer].astype(BF16),
                      w_mix_out[layer].astype(BF16), norm_ffn[layer].reshape(1, d))
        if layer % 2 == 0:
            h_mid, xn = _merge(*merge_args)
            h = _dense_ffn(xn, h_mid, dense_w_gate[i].astype(BF16), dense_w_up[i].astype(BF16),
                           dense_w_down[i].astype(BF16))
            if last:
                h = _final_norm(h, norm_final.reshape(1, d))
        else:
            wr = jnp.pad(router_w[i], ((0, 0), (0, LANES - N_EXPERTS)))
            wr_hi = wr.astype(BF16)
            wr_lo = (wr - wr_hi.astype(F32)).astype(BF16)
            h_mid, xn, gates = _merge(*merge_args, router=(wr_hi, wr_lo))
            h = _moe_ffn(xn, gates, h_mid, moe_w_gate[i].astype(BF16), moe_w_up[i].astype(BF16),
                         moe_w_down[i].astype(BF16), norm_final.reshape(1, d), last)
    return h.reshape(batch, seq, d)
```

```python
import functools

import jax
import jax.numpy as jnp
from jax import lax
from jax.experimental import pallas as pl
from jax.experimental.pallas import tpu as pltpu

F32 = jnp.float32
BF16 = jnp.bfloat16

D_MODEL = 1024
DEPTH = 2
CHUNK = 64
CONV_WIDTH = 1024
CONV_K = 3
MLA_HEADS = 16
QK_NOPE = 64
QK_ROPE = 32
V_DIM = 64
Q_LORA = 256
KV_LORA = 128
ROPE_THETA = 10000.0
N_EXPERTS = 8
TOP_K = 2
EPS = 1e-6
LOG2_E = 1.4426950408889634

LANES = 128
SUBLANES = 8
HEAD_PAD = LANES
HALF_ROPE = QK_ROPE // 2

C_GATE_B = 0
C_GATE_C = C_GATE_B + CONV_WIDTH
C_U = C_GATE_C + CONV_WIDTH
C_Q = C_U + CONV_WIDTH
C_KV = C_Q + Q_LORA
C_KR = C_KV + KV_LORA
C_KRS = C_KR + LANES
C_GL = C_KRS + LANES
IN_COLS_P = C_GL + 2 * D_MODEL

TM_FRONT = 256
TM_MIX = 512
TM_FFN = 512
TM_ROUTE = 512
TM_GROUP = 512
DMA_UNROLL = 8
TQ = 256
VMEM_LIMIT = 48 * 1024 * 1024


def _rms(x, g):
    inv = lax.rsqrt(jnp.mean(x * x, axis=-1, keepdims=True) + EPS)
    return (x * inv) * g


def _dot(a, b):
    return jnp.dot(a, b, preferred_element_type=F32)


def _const_spec(shape):
    nd = len(shape)
    return pl.BlockSpec(shape, lambda *_: (0,) * nd)


def _rope_kernel(pos_ref, invf_ref, cos_ref, sin_ref, nsin_ref):
    ang = pos_ref[...].astype(F32) * invf_ref[...]
    c = jnp.cos(ang)
    s = jnp.sin(ang)
    cos_ref[...] = c
    sin_ref[...] = s
    nsin_ref[...] = -s


def _rope_tables(positions):
    t = positions.size
    rows = t * HALF_ROPE // LANES
    pos_rep = jnp.repeat(positions.reshape(t), HALF_ROPE).reshape(rows, LANES)
    inv_freq = ROPE_THETA ** (-jnp.arange(0, QK_ROPE, 2, dtype=F32) / QK_ROPE)
    invf = jnp.tile(inv_freq, LANES // HALF_ROPE).reshape(1, LANES)
    blk = 512
    spec = pl.BlockSpec((blk, LANES), lambda i: (i, 0))
    cos, sin, nsin = pl.pallas_call(
        _rope_kernel,
        out_shape=[jax.ShapeDtypeStruct((rows, LANES), F32)] * 3,
        grid=(rows // blk,),
        in_specs=[spec, _const_spec((1, LANES))],
        out_specs=[spec] * 3,
        name="rope_tables",
    )(pos_rep, invf)
    cos = cos.reshape(t, HALF_ROPE)
    sin = sin.reshape(t, HALF_ROPE)
    nsin = nsin.reshape(t, HALF_ROPE)
    ones = jnp.ones((t, QK_NOPE), F32)
    zeros = jnp.zeros((t, QK_NOPE), F32)
    pad = jnp.zeros((t, HEAD_PAD - QK_NOPE - QK_ROPE), F32)
    c_tab = jnp.concatenate([ones, cos, cos, pad], axis=1)
    s_tab = jnp.concatenate([zeros, nsin, sin, pad], axis=1)
    return c_tab, s_tab


def _front_kernel(tiles_per_seq, x_ref, gmix_ref, win_ref, bgate_ref, convw_ref, wco_ref,
                  gq_ref, gkv_ref, wuq_ref, wuqs_ref, wuk_ref, wuv_ref, c_ref, s_ref,
                  q_ref, k_ref, v_ref, gaya_ref, gb_ref, zbuf):
    tm = x_ref.shape[0]
    i = pl.program_id(0)
    xn = _rms(x_ref[...], gmix_ref[...]).astype(BF16)

    def proj(lo, width):
        return _dot(xn, win_ref[:, lo:lo + width])

    z = proj(C_GATE_C, CONV_WIDTH) * proj(C_U, CONV_WIDTH)

    @pl.when(i % tiles_per_seq == 0)
    def _():
        zbuf[0:SUBLANES, :] = jnp.zeros((SUBLANES, CONV_WIDTH), F32)

    zbuf[SUBLANES:SUBLANES + tm, :] = z
    cw = convw_ref[...]
    conv = (cw[2:3, :] * z
            + cw[1:2, :] * zbuf[SUBLANES - 1:SUBLANES - 1 + tm, :]
            + cw[0:1, :] * zbuf[SUBLANES - 2:SUBLANES - 2 + tm, :])
    zbuf[0:SUBLANES, :] = zbuf[tm:tm + SUBLANES, :]
    y_a = _dot((proj(C_GATE_B, CONV_WIDTH) * conv).astype(BF16), wco_ref[...])

    gate = jax.nn.sigmoid(proj(C_GL, 2 * D_MODEL) + bgate_ref[...])
    gaya_ref[...] = (gate[:, :D_MODEL] * y_a).astype(BF16)
    gb_ref[...] = gate[:, D_MODEL:].astype(BF16)

    c_tab = c_ref[...]
    s_tab = s_ref[...]
    c_all = jnp.tile(c_tab, (1, MLA_HEADS))
    s_all = jnp.tile(s_tab, (1, MLA_HEADS))
    cqn = _rms(proj(C_Q, Q_LORA), gq_ref[...]).astype(BF16)
    q = _dot(cqn, wuq_ref[...]) * c_all + _dot(cqn, wuqs_ref[...]) * s_all
    q_ref[...] = (q * (LOG2_E * (QK_NOPE + QK_ROPE) ** -0.5)).astype(BF16)

    ckvn = _rms(proj(C_KV, KV_LORA), gkv_ref[...]).astype(BF16)
    k_rot = proj(C_KR, LANES) * c_tab + proj(C_KRS, LANES) * s_tab
    k_ref[...] = (_dot(ckvn, wuk_ref[...]) + jnp.tile(k_rot, (1, MLA_HEADS))).astype(BF16)
    v_ref[...] = _dot(ckvn, wuv_ref[...]).astype(BF16)


def _front(h, seq, gmix, win, bgate, convw, wco, gq, gkv, wuq, wuqs, wuk, wuv, c_tab, s_tab):
    t = h.shape[0]
    tm = TM_FRONT
    hp = MLA_HEADS * HEAD_PAD
    row = lambda w: pl.BlockSpec((tm, w), lambda i: (i, 0))
    return pl.pallas_call(
        functools.partial(_front_kernel, seq // tm),
        out_shape=[jax.ShapeDtypeStruct((t, hp), BF16),
                   jax.ShapeDtypeStruct((t, hp), BF16),
                   jax.ShapeDtypeStruct((t, MLA_HEADS * V_DIM), BF16),
                   jax.ShapeDtypeStruct((t, D_MODEL), BF16),
                   jax.ShapeDtypeStruct((t, D_MODEL), BF16)],
        grid=(t // tm,),
        in_specs=[row(D_MODEL), _const_spec(gmix.shape), _const_spec(win.shape),
                  _const_spec(bgate.shape), _const_spec(convw.shape), _const_spec(wco.shape),
                  _const_spec(gq.shape), _const_spec(gkv.shape), _const_spec(wuq.shape),
                  _const_spec(wuqs.shape), _const_spec(wuk.shape), _const_spec(wuv.shape),
                  row(LANES), row(LANES)],
        out_specs=[row(hp), row(hp), row(MLA_HEADS * V_DIM), row(D_MODEL), row(D_MODEL)],
        scratch_shapes=[pltpu.VMEM((tm + SUBLANES, CONV_WIDTH), F32)],
        compiler_params=pltpu.CompilerParams(
            dimension_semantics=("arbitrary",), vmem_limit_bytes=VMEM_LIMIT),
        name="mixer_front",
    )(h, gmix, win, bgate, convw, wco, gq, gkv, wuq, wuqs, wuk, wuv, c_tab, s_tab)


def _attn_kernel(tq, q_ref, k_ref, v_ref, o_ref):
    seq = q_ref.shape[0]
    row_chunk = lax.broadcasted_iota(jnp.int32, (tq, tq), 0) // CHUNK
    col_chunk = lax.broadcasted_iota(jnp.int32, (tq, tq), 1) // CHUNK
    diag_mask = col_chunk <= row_chunk
    lane = lax.broadcasted_iota(jnp.int32, (tq, 2 * V_DIM), 1)
    contract_last = (((1,), (1,)), ((), ()))

    for qi in range(seq // tq):
        q_lo = qi * tq
        outs = []
        for hh in range(2):
            cols = slice(hh * HEAD_PAD, (hh + 1) * HEAD_PAD)
            q = q_ref[q_lo:q_lo + tq, cols]
            s_diag = lax.dot_general(q, k_ref[q_lo:q_lo + tq, cols], contract_last,
                                     preferred_element_type=F32)
            s_diag = jnp.where(diag_mask, s_diag, -jnp.inf)
            m = jnp.max(s_diag, axis=-1, keepdims=True)
            if qi:
                s_past = lax.dot_general(q, k_ref[0:q_lo, cols], contract_last,
                                         preferred_element_type=F32)
                m = jnp.maximum(m, jnp.max(s_past, axis=-1, keepdims=True))
            p_diag = jnp.exp2(s_diag - m)
            l = jnp.sum(p_diag, axis=-1, keepdims=True)
            acc = _dot(p_diag.astype(BF16), v_ref[q_lo:q_lo + tq, :])
            if qi:
                p_past = jnp.exp2(s_past - m)
                l = l + jnp.sum(p_past, axis=-1, keepdims=True)
                acc = acc + _dot(p_past.astype(BF16), v_ref[0:q_lo, :])
            outs.append(acc / l)
        o_ref[q_lo:q_lo + tq, :] = jnp.where(lane < V_DIM, outs[0], outs[1]).astype(o_ref.dtype)


def _attention(q, k, v, batch, seq):
    t = q.shape[0]
    pairs = MLA_HEADS // 2
    return pl.pallas_call(
        functools.partial(_attn_kernel, TQ),
        out_shape=jax.ShapeDtypeStruct((t, MLA_HEADS * V_DIM), BF16),
        grid=(batch, pairs),
        in_specs=[pl.BlockSpec((seq, 2 * HEAD_PAD), lambda b, p: (b, p)),
                  pl.BlockSpec((seq, 2 * HEAD_PAD), lambda b, p: (b, p)),
                  pl.BlockSpec((seq, 2 * V_DIM), lambda b, p: (b, p))],
        out_specs=pl.BlockSpec((seq, 2 * V_DIM), lambda b, p: (b, p)),
        compiler_params=pltpu.CompilerParams(
            dimension_semantics=("parallel", "parallel"), vmem_limit_bytes=VMEM_LIMIT),
        name="chunk_attention",
    )(q, k, v)


def _merge_body(o_ref, gaya_ref, gb_ref, h_ref, wo_ref, wmix_ref, gffn_ref, hmid_ref):
    y_b = _dot(o_ref[...], wo_ref[...])
    merged = gaya_ref[...].astype(F32) + gb_ref[...].astype(F32) * y_b
    h_mid = h_ref[...] + _dot(merged.astype(BF16), wmix_ref[...])
    hmid_ref[...] = h_mid
    return _rms(h_mid, gffn_ref[...])


def _merge_kernel(o_ref, gaya_ref, gb_ref, h_ref, wo_ref, wmix_ref, gffn_ref, hmid_ref, xn_ref):
    xn = _merge_body(o_ref, gaya_ref, gb_ref, h_ref, wo_ref, wmix_ref, gffn_ref, hmid_ref)
    xn_ref[...] = xn.astype(BF16)


R_W1, R_W2, R_E1, R_E2, R_RANK1, R_RANK2 = range(6)


def _merge_router_kernel(o_ref, gaya_ref, gb_ref, h_ref, wo_ref, wmix_ref, gffn_ref,
                         wrh_ref, wrl_ref, hmid_ref, xp_ref, route_ref, counts_ref):
    xn = _merge_body(o_ref, gaya_ref, gb_ref, h_ref, wo_ref, wmix_ref, gffn_ref, hmid_ref)
    tm = xn.shape[0]
    half = D_MODEL // 2

    @pl.when(pl.program_id(0) == 0)
    def _():
        counts_ref[...] = jnp.zeros(counts_ref.shape, F32)

    x_hi = xn.astype(BF16)
    bits = lax.bitcast_convert_type(x_hi.astype(F32), jnp.uint32)
    xp_ref[...] = bits[:, :half] | (bits[:, half:] >> 16)

    x_lo = (xn - x_hi.astype(F32)).astype(BF16)
    logits = _dot(x_hi, wrh_ref[...]) + (_dot(x_hi, wrl_ref[...]) + _dot(x_lo, wrh_ref[...]))
    lane = lax.broadcasted_iota(jnp.int32, logits.shape, 1)
    lg = jnp.where(lane < N_EXPERTS, logits, -jnp.inf)
    m1 = jnp.max(lg, axis=-1, keepdims=True)
    i1 = jnp.min(jnp.where(lg == m1, lane, LANES), axis=-1, keepdims=True)
    lg2 = jnp.where(lane == i1, -jnp.inf, lg)
    m2 = jnp.max(lg2, axis=-1, keepdims=True)
    i2 = jnp.min(jnp.where(lg2 == m2, lane, LANES), axis=-1, keepdims=True)
    e2 = jnp.exp(m2 - m1)
    den = 1.0 + e2

    onehot = jnp.where(lane == i1, 1.0, jnp.where(lane == i2, 1.0, 0.0))
    rows = lax.broadcasted_iota(jnp.int32, (tm, tm), 0)
    cols = lax.broadcasted_iota(jnp.int32, (tm, tm), 1)
    tri = jnp.where(rows > cols, 1.0, 0.0).astype(BF16)
    pos = _dot(tri, onehot.astype(BF16)) + counts_ref[...]
    rank1 = jnp.sum(jnp.where(lane == i1, pos, 0.0), axis=-1, keepdims=True)
    rank2 = jnp.sum(jnp.where(lane == i2, pos, 0.0), axis=-1, keepdims=True)
    counts_ref[...] += jnp.sum(onehot, axis=0, keepdims=True)

    record = jnp.zeros(logits.shape, F32)
    for lane_id, val in ((R_W1, 1.0 / den), (R_W2, e2 / den), (R_E1, i1.astype(F32)),
                         (R_E2, i2.astype(F32)), (R_RANK1, rank1), (R_RANK2, rank2)):
        record = jnp.where(lane == lane_id, val, record)
    route_ref[...] = record


def _merge(o, gaya, gb, h, wo, wmix, gffn, router=None):
    t = h.shape[0]
    tm = TM_MIX
    row = lambda w: pl.BlockSpec((tm, w), lambda i: (i, 0))
    in_specs = [row(D_MODEL)] * 4 + [_const_spec(wo.shape), _const_spec(wmix.shape),
                                     _const_spec(gffn.shape)]
    args = [o, gaya, gb, h, wo, wmix, gffn]
    if router is None:
        body = _merge_kernel
        out_shape = [jax.ShapeDtypeStruct((t, D_MODEL), F32),
                     jax.ShapeDtypeStruct((t, D_MODEL), BF16)]
        out_specs = [row(D_MODEL), row(D_MODEL)]
    else:
        body = _merge_router_kernel
        in_specs += [_const_spec(router[0].shape), _const_spec(router[1].shape)]
        args += list(router)
        out_shape = [jax.ShapeDtypeStruct((t, D_MODEL), F32),
                     jax.ShapeDtypeStruct((t, D_MODEL // 2), jnp.uint32),
                     jax.ShapeDtypeStruct((t, LANES), F32),
                     jax.ShapeDtypeStruct((1, LANES), F32)]
        out_specs = [row(D_MODEL), row(D_MODEL // 2), row(LANES), _const_spec((1, LANES))]
    return pl.pallas_call(
        body, out_shape=out_shape, grid=(t // tm,), in_specs=in_specs, out_specs=out_specs,
        compiler_params=pltpu.CompilerParams(
            dimension_semantics=("arbitrary",), vmem_limit_bytes=VMEM_LIMIT),
        name="merge_router" if router is not None else "merge",
    )(*args)


def _swiglu(x, wg, wu, wd):
    g = _dot(x, wg)
    u = _dot(x, wu)
    return _dot((g * jax.nn.sigmoid(g) * u).astype(BF16), wd)


def _dense_ffn_kernel(x_ref, hmid_ref, wg_ref, wu_ref, wd_ref, out_ref):
    f = pl.program_id(1)

    @pl.when(f == 0)
    def _():
        out_ref[...] = hmid_ref[...]

    out_ref[...] += _swiglu(x_ref[...], wg_ref[...], wu_ref[...], wd_ref[...])


def _dense_ffn(xn, h_mid, wg, wu, wd):
    t = xn.shape[0]
    tm = TM_FFN
    d_ff = wg.shape[1]
    nf = 2
    tf = d_ff // nf
    row = pl.BlockSpec((tm, D_MODEL), lambda i, f: (i, 0))
    return pl.pallas_call(
        _dense_ffn_kernel,
        out_shape=jax.ShapeDtypeStruct((t, D_MODEL), F32),
        grid=(t // tm, nf),
        in_specs=[row, row,
                  pl.BlockSpec((D_MODEL, tf), lambda i, f: (0, f)),
                  pl.BlockSpec((D_MODEL, tf), lambda i, f: (0, f)),
                  pl.BlockSpec((tf, D_MODEL), lambda i, f: (f, 0))],
        out_specs=row,
        compiler_params=pltpu.CompilerParams(
            dimension_semantics=("parallel", "arbitrary"), vmem_limit_bytes=VMEM_LIMIT),
        name="dense_ffn",
    )(xn, h_mid, wg, wu, wd)


def _row_copy(src_ref, src_row, dst_ref, dst_row, sem):
    return pltpu.make_async_copy(src_ref.at[pl.ds(src_row, 1)], dst_ref.at[pl.ds(dst_row, 1)], sem)


def _drain_row_copies(src_ref, dst_ref, sem, n):
    def body(_, carry):
        _row_copy(src_ref, 0, dst_ref, 0, sem).wait()
        return carry
    lax.fori_loop(0, n, body, 0, unroll=DMA_UNROLL)


def _dispatch_kernel(slots_ref, x_ref, _, xs_ref, sem):
    tm = x_ref.shape[0]

    def issue(r, carry):
        for k in range(TOP_K):
            _row_copy(x_ref, r, xs_ref, slots_ref[0, 0, TOP_K * r + k], sem).start()
        return carry

    lax.fori_loop(0, tm, issue, 0, unroll=DMA_UNROLL)
    _drain_row_copies(x_ref, xs_ref, sem, TOP_K * tm)


def _dispatch(xp, slots, n_rows):
    t, w = xp.shape
    tm = TM_ROUTE
    xs0 = jnp.zeros((n_rows, w), xp.dtype)
    return pl.pallas_call(
        _dispatch_kernel,
        out_shape=jax.ShapeDtypeStruct((n_rows, w), xp.dtype),
        grid=(t // tm,),
        in_specs=[pl.BlockSpec((1, 1, TOP_K * tm), lambda i: (i, 0, 0), memory_space=pltpu.SMEM),
                  pl.BlockSpec((tm, w), lambda i: (i, 0)),
                  pl.BlockSpec(memory_space=pl.ANY)],
        out_specs=pl.BlockSpec(memory_space=pl.ANY),
        scratch_shapes=[pltpu.SemaphoreType.DMA],
        input_output_aliases={2: 0},
        compiler_params=pltpu.CompilerParams(dimension_semantics=("arbitrary",)),
        name="moe_dispatch",
    )(slots, xp, xs0)


def _group_ffn_kernel(te_ref, tb_ref, nv_ref, x_ref, wg_ref, wu_ref, wd_ref, y_ref):
    del te_ref, tb_ref
    live = pl.program_id(0) < nv_ref[0]

    @pl.when(live)
    def _():
        packed = x_ref[...]
        hi = lax.bitcast_convert_type(packed & jnp.uint32(0xFFFF0000), F32)
        lo = lax.bitcast_convert_type(packed << 16, F32)
        x = jnp.concatenate([hi, lo], axis=1).astype(BF16)
        y_ref[...] = _swiglu(x, wg_ref[...], wu_ref[...], wd_ref[...])

    @pl.when(jnp.logical_not(live))
    def _():
        y_ref[...] = jnp.zeros(y_ref.shape, y_ref.dtype)


def _group_ffn(xs, tile_expert, tile_block, n_valid, wg, wu, wd):
    n_rows, w = xs.shape
    tm = TM_GROUP
    _, _, d_ff = wg.shape
    return pl.pallas_call(
        _group_ffn_kernel,
        out_shape=jax.ShapeDtypeStruct((n_rows, D_MODEL), F32),
        grid_spec=pltpu.PrefetchScalarGridSpec(
            num_scalar_prefetch=3,
            grid=(n_rows // tm,),
            in_specs=[pl.BlockSpec((tm, w), lambda i, te, tb, nv: (tb[i], 0)),
                      pl.BlockSpec((None, D_MODEL, d_ff), lambda i, te, tb, nv: (te[i], 0, 0)),
                      pl.BlockSpec((None, D_MODEL, d_ff), lambda i, te, tb, nv: (te[i], 0, 0)),
                      pl.BlockSpec((None, d_ff, D_MODEL), lambda i, te, tb, nv: (te[i], 0, 0))],
            out_specs=pl.BlockSpec((tm, D_MODEL), lambda i, te, tb, nv: (i, 0))),
        compiler_params=pltpu.CompilerParams(
            dimension_semantics=("arbitrary",), vmem_limit_bytes=VMEM_LIMIT),
        name="moe_group_ffn",
    )(tile_expert, tile_block, n_valid, xs, wg, wu, wd)


def _combine_kernel(final_norm, slots_ref, route_ref, hmid_ref, gfin_ref, ys_ref, out_ref,
                    ybuf, sem):
    tm = hmid_ref.shape[0]

    def issue(r, carry):
        for k in range(TOP_K):
            _row_copy(ys_ref, slots_ref[0, 0, TOP_K * r + k], ybuf.at[k], r, sem).start()
        return carry

    lax.fori_loop(0, tm, issue, 0, unroll=DMA_UNROLL)
    _drain_row_copies(ys_ref, ybuf.at[0], sem, TOP_K * tm)
    route = route_ref[...]
    h = (hmid_ref[...] + route[:, R_W1:R_W1 + 1] * ybuf[0]) + route[:, R_W2:R_W2 + 1] * ybuf[1]
    out_ref[...] = _rms(h, gfin_ref[...]) if final_norm else h


def _combine(ys, slots, route, h_mid, gfin, final_norm):
    t = h_mid.shape[0]
    tm = TM_ROUTE
    row = lambda w: pl.BlockSpec((tm, w), lambda i: (i, 0))
    return pl.pallas_call(
        functools.partial(_combine_kernel, final_norm),
        out_shape=jax.ShapeDtypeStruct((t, D_MODEL), F32),
        grid=(t // tm,),
        in_specs=[pl.BlockSpec((1, 1, TOP_K * tm), lambda i: (i, 0, 0), memory_space=pltpu.SMEM),
                  row(LANES), row(D_MODEL), _const_spec(gfin.shape),
                  pl.BlockSpec(memory_space=pl.ANY)],
        out_specs=row(D_MODEL),
        scratch_shapes=[pltpu.VMEM((TOP_K, tm, D_MODEL), F32), pltpu.SemaphoreType.DMA],
        compiler_params=pltpu.CompilerParams(dimension_semantics=("arbitrary",)),
        name="moe_combine",
    )(slots, route, h_mid, gfin, ys)


def _moe_ffn(xp, route, counts, h_mid, wg, wu, wd, gfin, final_norm):
    t = h_mid.shape[0]
    tm = TM_GROUP
    n_tiles = TOP_K * t // tm + N_EXPERTS
    e1 = route[:, R_E1].astype(jnp.int32)
    e2 = route[:, R_E2].astype(jnp.int32)
    rank1 = route[:, R_RANK1].astype(jnp.int32)
    rank2 = route[:, R_RANK2].astype(jnp.int32)
    tiles_per_expert = (counts[0, :N_EXPERTS].astype(jnp.int32) + tm - 1) // tm
    tile_end = jnp.cumsum(tiles_per_expert)
    row_start = (tile_end - tiles_per_expert) * tm
    slots = jnp.stack([row_start[e1] + rank1, row_start[e2] + rank2], axis=1)
    slots = slots.reshape(t // TM_ROUTE, 1, TOP_K * TM_ROUTE)
    n_valid = tile_end[-1:]
    tile_block = jnp.minimum(jnp.arange(n_tiles, dtype=jnp.int32), n_valid - 1)
    tile_expert = jnp.minimum(jnp.searchsorted(tile_end, tile_block, side="right"),
                              N_EXPERTS - 1).astype(jnp.int32)
    xs = _dispatch(xp, slots, n_tiles * tm)
    ys = _group_ffn(xs, tile_expert, tile_block, n_valid, wg, wu, wd)
    return _combine(ys, slots, route, h_mid, gfin, final_norm)


def _final_norm_kernel(h_ref, g_ref, out_ref):
    out_ref[...] = _rms(h_ref[...], g_ref[...])


def _final_norm(h, g):
    t = h.shape[0]
    row = pl.BlockSpec((TM_FFN, D_MODEL), lambda i: (i, 0))
    return pl.pallas_call(
        _final_norm_kernel, out_shape=jax.ShapeDtypeStruct((t, D_MODEL), F32),
        grid=(t // TM_FFN,), in_specs=[row, _const_spec(g.shape)], out_specs=row,
        name="final_norm",
    )(h, g)


def _pack_w_in(w):
    d = w.shape[0]
    kr0 = 3 * CONV_WIDTH + Q_LORA + KV_LORA
    kr_a = w[:, kr0:kr0 + HALF_ROPE]
    kr_b = w[:, kr0 + HALF_ROPE:kr0 + QK_ROPE]
    z_lo = jnp.zeros((d, QK_NOPE), w.dtype)
    z_hi = jnp.zeros((d, HEAD_PAD - QK_NOPE - QK_ROPE), w.dtype)
    return jnp.concatenate(
        [w[:, :kr0], z_lo, kr_a, kr_b, z_hi, z_lo, kr_b, kr_a, z_hi, w[:, kr0 + QK_ROPE:]], axis=1)


def _pack_w_uq(w):
    w3 = w.reshape(Q_LORA, MLA_HEADS, QK_NOPE + QK_ROPE)
    nope = w3[..., :QK_NOPE]
    r_a = w3[..., QK_NOPE:QK_NOPE + HALF_ROPE]
    r_b = w3[..., QK_NOPE + HALF_ROPE:]
    pad = jnp.zeros((Q_LORA, MLA_HEADS, HEAD_PAD - QK_NOPE - QK_ROPE), w.dtype)
    plain = jnp.concatenate([nope, r_a, r_b, pad], axis=-1)
    swapped = jnp.concatenate([jnp.zeros_like(nope), r_b, r_a, pad], axis=-1)
    return (plain.reshape(Q_LORA, MLA_HEADS * HEAD_PAD),
            swapped.reshape(Q_LORA, MLA_HEADS * HEAD_PAD))


def _pack_w_ukv(w):
    w3 = w.reshape(KV_LORA, MLA_HEADS, QK_NOPE + V_DIM)
    k = jnp.concatenate([w3[..., :QK_NOPE],
                         jnp.zeros((KV_LORA, MLA_HEADS, HEAD_PAD - QK_NOPE), w.dtype)], axis=-1)
    return (k.reshape(KV_LORA, MLA_HEADS * HEAD_PAD),
            w3[..., QK_NOPE:].reshape(KV_LORA, MLA_HEADS * V_DIM))


def kernel(x, positions, norm_mix, w_in, b_gate, conv_w, w_conv_out, norm_q, norm_kv, w_uq, w_ukv,
           w_attn_out, w_mix_out, norm_ffn, dense_w_gate, dense_w_up, dense_w_down, router_w,
           moe_w_gate, moe_w_up, moe_w_down, norm_final):
    batch, seq, d = x.shape
    t = batch * seq
    c_tab, s_tab = _rope_tables(positions)
    h = x.reshape(t, d)
    for layer in range(DEPTH):
        wuq, wuqs = _pack_w_uq(w_uq[layer])
        wuk, wuv = _pack_w_ukv(w_ukv[layer])
        q, k, v, gaya, gb = _front(
            h, seq, norm_mix[layer].reshape(1, d), _pack_w_in(w_in[layer]).astype(BF16),
            b_gate[layer].reshape(1, 2 * d), conv_w[layer].reshape(CONV_K, CONV_WIDTH),
            w_conv_out[layer].astype(BF16), norm_q[layer].reshape(1, Q_LORA),
            norm_kv[layer].reshape(1, KV_LORA), wuq.astype(BF16), wuqs.astype(BF16),
            wuk.astype(BF16), wuv.astype(BF16), c_tab, s_tab)
        o = _attention(q, k, v, batch, seq)
        i = layer // 2
        last = layer == DEPTH - 1
        merge_args = (o, gaya, gb, h, w_attn_out[layer].astype(BF16),
                      w_mix_out[layer].astype(BF16), norm_ffn[layer].reshape(1, d))
        if layer % 2 == 0:
            h_mid, xn = _merge(*merge_args)
            h = _dense_ffn(xn, h_mid, dense_w_gate[i].astype(BF16), dense_w_up[i].astype(BF16),
                           dense_w_down[i].astype(BF16))
            if last:
                h = _final_norm(h, norm_final.reshape(1, d))
        else:
            wr = jnp.pad(router_w[i], ((0, 0), (0, LANES - N_EXPERTS)))
            wr_hi = wr.astype(BF16)
            wr_lo = (wr - wr_hi.astype(F32)).astype(BF16)
            h_mid, xp, route, counts = _merge(*merge_args, router=(wr_hi, wr_lo))
            h = _moe_ffn(xp, route, counts, h_mid, moe_w_gate[i].astype(BF16),
                         moe_w_up[i].astype(BF16), moe_w_down[i].astype(BF16),
                         norm_final.reshape(1, d), last)
    return h.reshape(batch, seq, d)
```

```python
import functools

import jax
import jax.numpy as jnp
from jax import lax
from jax.experimental import pallas as pl
from jax.experimental.pallas import tpu as pltpu

F32 = jnp.float32
BF16 = jnp.bfloat16

D_MODEL = 1024
DEPTH = 2
CHUNK = 64
CONV_WIDTH = 1024
CONV_K = 3
MLA_HEADS = 16
QK_NOPE = 64
QK_ROPE = 32
V_DIM = 64
Q_LORA = 256
KV_LORA = 128
ROPE_THETA = 10000.0
N_EXPERTS = 8
TOP_K = 2
EPS = 1e-6
LOG2_E = 1.4426950408889634

LANES = 128
SUBLANES = 8
HEAD_PAD = LANES
HALF_ROPE = QK_ROPE // 2

C_GATE_B = 0
C_GATE_C = C_GATE_B + CONV_WIDTH
C_U = C_GATE_C + CONV_WIDTH
C_Q = C_U + CONV_WIDTH
C_KV = C_Q + Q_LORA
C_KR = C_KV + KV_LORA
C_KRS = C_KR + LANES
C_GL = C_KRS + LANES
IN_COLS_P = C_GL + 2 * D_MODEL

TM_FRONT = 512
TM_MIX = 512
TM_FFN = 512
TM_ROUTE = 512
TM_GROUP = 512
DMA_UNROLL = 8
TQ = 512
VMEM_LIMIT = 48 * 1024 * 1024


def _rms(x, g):
    inv = lax.rsqrt(jnp.mean(x * x, axis=-1, keepdims=True) + EPS)
    return (x * inv) * g


def _dot(a, b):
    return jnp.dot(a, b, preferred_element_type=F32)


def _const_spec(shape, single_buffer=False):
    nd = len(shape)
    mode = pl.Buffered(1) if single_buffer else None
    return pl.BlockSpec(shape, lambda *_: (0,) * nd, pipeline_mode=mode)


def _rope_kernel(pos_ref, invf_ref, cos_ref, sin_ref, nsin_ref):
    ang = pos_ref[...].astype(F32) * invf_ref[...]
    c = jnp.cos(ang)
    s = jnp.sin(ang)
    cos_ref[...] = c
    sin_ref[...] = s
    nsin_ref[...] = -s


def _rope_tables(positions):
    t = positions.size
    rows = t * HALF_ROPE // LANES
    pos_rep = jnp.repeat(positions.reshape(t), HALF_ROPE).reshape(rows, LANES)
    inv_freq = ROPE_THETA ** (-jnp.arange(0, QK_ROPE, 2, dtype=F32) / QK_ROPE)
    invf = jnp.tile(inv_freq, LANES // HALF_ROPE).reshape(1, LANES)
    blk = 512
    spec = pl.BlockSpec((blk, LANES), lambda i: (i, 0))
    cos, sin, nsin = pl.pallas_call(
        _rope_kernel,
        out_shape=[jax.ShapeDtypeStruct((rows, LANES), F32)] * 3,
        grid=(rows // blk,),
        in_specs=[spec, _const_spec((1, LANES))],
        out_specs=[spec] * 3,
        name="rope_tables",
    )(pos_rep, invf)
    cos = cos.reshape(t, HALF_ROPE)
    sin = sin.reshape(t, HALF_ROPE)
    nsin = nsin.reshape(t, HALF_ROPE)
    ones = jnp.ones((t, QK_NOPE), F32)
    zeros = jnp.zeros((t, QK_NOPE), F32)
    pad = jnp.zeros((t, HEAD_PAD - QK_NOPE - QK_ROPE), F32)
    c_tab = jnp.concatenate([ones, cos, cos, pad], axis=1)
    s_tab = jnp.concatenate([zeros, nsin, sin, pad], axis=1)
    return c_tab, s_tab


def _front_kernel(tiles_per_seq, x_ref, gmix_ref, win_ref, bgate_ref, convw_ref, wco_ref,
                  gq_ref, gkv_ref, wuq_ref, wuqs_ref, wuk_ref, wuv_ref, c_ref, s_ref,
                  q_ref, k_ref, v_ref, gaya_ref, gb_ref, zbuf):
    tm = x_ref.shape[0]
    i = pl.program_id(0)
    xn = _rms(x_ref[...], gmix_ref[...]).astype(BF16)

    def proj(lo, width):
        return _dot(xn, win_ref[:, lo:lo + width])

    z = proj(C_GATE_C, CONV_WIDTH) * proj(C_U, CONV_WIDTH)

    @pl.when(i % tiles_per_seq == 0)
    def _():
        zbuf[0:SUBLANES, :] = jnp.zeros((SUBLANES, CONV_WIDTH), F32)

    zbuf[SUBLANES:SUBLANES + tm, :] = z
    cw = convw_ref[...]
    conv = (cw[2:3, :] * z
            + cw[1:2, :] * zbuf[SUBLANES - 1:SUBLANES - 1 + tm, :]
            + cw[0:1, :] * zbuf[SUBLANES - 2:SUBLANES - 2 + tm, :])
    zbuf[0:SUBLANES, :] = zbuf[tm:tm + SUBLANES, :]
    y_a = _dot((proj(C_GATE_B, CONV_WIDTH) * conv).astype(BF16), wco_ref[...])

    gate = jax.nn.sigmoid(proj(C_GL, 2 * D_MODEL) + bgate_ref[...])
    gaya_ref[...] = (gate[:, :D_MODEL] * y_a).astype(BF16)
    gb_ref[...] = gate[:, D_MODEL:].astype(BF16)

    c_tab = c_ref[...]
    s_tab = s_ref[...]
    c_all = jnp.tile(c_tab, (1, MLA_HEADS))
    s_all = jnp.tile(s_tab, (1, MLA_HEADS))
    cqn = _rms(proj(C_Q, Q_LORA), gq_ref[...]).astype(BF16)
    q = _dot(cqn, wuq_ref[...]) * c_all + _dot(cqn, wuqs_ref[...]) * s_all
    q_ref[...] = (q * (LOG2_E * (QK_NOPE + QK_ROPE) ** -0.5)).astype(BF16)

    ckvn = _rms(proj(C_KV, KV_LORA), gkv_ref[...]).astype(BF16)
    k_rot = proj(C_KR, LANES) * c_tab + proj(C_KRS, LANES) * s_tab
    k_ref[...] = (_dot(ckvn, wuk_ref[...]) + jnp.tile(k_rot, (1, MLA_HEADS))).astype(BF16)
    v_ref[...] = _dot(ckvn, wuv_ref[...]).astype(BF16)


def _front(h, seq, gmix, win, bgate, convw, wco, gq, gkv, wuq, wuqs, wuk, wuv, c_tab, s_tab):
    t = h.shape[0]
    tm = TM_FRONT
    hp = MLA_HEADS * HEAD_PAD
    row = lambda w: pl.BlockSpec((tm, w), lambda i: (i, 0))
    return pl.pallas_call(
        functools.partial(_front_kernel, seq // tm),
        out_shape=[jax.ShapeDtypeStruct((t, hp), BF16),
                   jax.ShapeDtypeStruct((t, hp), BF16),
                   jax.ShapeDtypeStruct((t, MLA_HEADS * V_DIM), BF16),
                   jax.ShapeDtypeStruct((t, D_MODEL), BF16),
                   jax.ShapeDtypeStruct((t, D_MODEL), BF16)],
        grid=(t // tm,),
        in_specs=[row(D_MODEL)]
        + [_const_spec(a.shape, single_buffer=True)
           for a in (gmix, win, bgate, convw, wco, gq, gkv, wuq, wuqs, wuk, wuv)]
        + [row(LANES), row(LANES)],
        out_specs=[row(hp), row(hp), row(MLA_HEADS * V_DIM), row(D_MODEL), row(D_MODEL)],
        scratch_shapes=[pltpu.VMEM((tm + SUBLANES, CONV_WIDTH), F32)],
        compiler_params=pltpu.CompilerParams(
            dimension_semantics=("arbitrary",), vmem_limit_bytes=VMEM_LIMIT),
        name="mixer_front",
    )(h, gmix, win, bgate, convw, wco, gq, gkv, wuq, wuqs, wuk, wuv, c_tab, s_tab)


def _attn_kernel(tq, q_ref, k_ref, v_ref, o_ref):
    seq = q_ref.shape[0]
    row_chunk = lax.broadcasted_iota(jnp.int32, (tq, tq), 0) // CHUNK
    col_chunk = lax.broadcasted_iota(jnp.int32, (tq, tq), 1) // CHUNK
    diag_mask = col_chunk <= row_chunk
    lane = lax.broadcasted_iota(jnp.int32, (tq, 2 * V_DIM), 1)
    v_lane = lax.broadcasted_iota(jnp.int32, v_ref.shape, 1)
    contract_last = (((1,), (1,)), ((), ()))
    v_pair = v_ref[...]
    one = jnp.ones((), v_pair.dtype)
    v_ext = [jnp.where(v_lane < V_DIM, v_pair, one), jnp.where(v_lane < V_DIM, one, v_pair)]

    for qi in range(seq // tq):
        q_lo = qi * tq
        outs = []
        for hh in range(2):
            cols = slice(hh * HEAD_PAD, (hh + 1) * HEAD_PAD)
            q = q_ref[q_lo:q_lo + tq, cols]
            s_diag = lax.dot_general(q, k_ref[q_lo:q_lo + tq, cols], contract_last,
                                     preferred_element_type=F32)
            s_diag = jnp.where(diag_mask, s_diag, -jnp.inf)
            m = jnp.max(s_diag, axis=-1, keepdims=True)
            if qi:
                s_past = lax.dot_general(q, k_ref[0:q_lo, cols], contract_last,
                                         preferred_element_type=F32)
                m = jnp.maximum(m, jnp.max(s_past, axis=-1, keepdims=True))
            acc = _dot(jnp.exp2(s_diag - m).astype(BF16), v_ext[hh][q_lo:q_lo + tq, :])
            if qi:
                acc = acc + _dot(jnp.exp2(s_past - m).astype(BF16), v_ext[hh][0:q_lo, :])
            outs.append(acc / pltpu.roll(acc, V_DIM, axis=1))
        o_ref[q_lo:q_lo + tq, :] = jnp.where(lane < V_DIM, outs[0], outs[1]).astype(o_ref.dtype)


def _attention(q, k, v, batch, seq):
    t = q.shape[0]
    pairs = MLA_HEADS // 2
    return pl.pallas_call(
        functools.partial(_attn_kernel, TQ),
        out_shape=jax.ShapeDtypeStruct((t, MLA_HEADS * V_DIM), BF16),
        grid=(batch, pairs),
        in_specs=[pl.BlockSpec((seq, 2 * HEAD_PAD), lambda b, p: (b, p)),
                  pl.BlockSpec((seq, 2 * HEAD_PAD), lambda b, p: (b, p)),
                  pl.BlockSpec((seq, 2 * V_DIM), lambda b, p: (b, p))],
        out_specs=pl.BlockSpec((seq, 2 * V_DIM), lambda b, p: (b, p)),
        compiler_params=pltpu.CompilerParams(
            dimension_semantics=("parallel", "parallel"), vmem_limit_bytes=VMEM_LIMIT),
        name="chunk_attention",
    )(q, k, v)


def _merge_body(o_ref, gaya_ref, gb_ref, h_ref, wo_ref, wmix_ref, gffn_ref, hmid_ref):
    y_b = _dot(o_ref[...], wo_ref[...])
    merged = gaya_ref[...].astype(F32) + gb_ref[...].astype(F32) * y_b
    h_mid = h_ref[...] + _dot(merged.astype(BF16), wmix_ref[...])
    hmid_ref[...] = h_mid
    return _rms(h_mid, gffn_ref[...])


def _merge_kernel(o_ref, gaya_ref, gb_ref, h_ref, wo_ref, wmix_ref, gffn_ref, hmid_ref, xn_ref):
    xn = _merge_body(o_ref, gaya_ref, gb_ref, h_ref, wo_ref, wmix_ref, gffn_ref, hmid_ref)
    xn_ref[...] = xn.astype(BF16)


R_W1, R_W2, R_E1, R_E2, R_RANK1, R_RANK2 = range(6)


def _merge_router_kernel(o_ref, gaya_ref, gb_ref, h_ref, wo_ref, wmix_ref, gffn_ref,
                         wr_ref, hmid_ref, xp_ref, route_ref, counts_ref):
    xn = _merge_body(o_ref, gaya_ref, gb_ref, h_ref, wo_ref, wmix_ref, gffn_ref, hmid_ref)
    tm = xn.shape[0]
    half = D_MODEL // 2

    @pl.when(pl.program_id(0) == 0)
    def _():
        counts_ref[...] = jnp.zeros(counts_ref.shape, F32)

    x_hi = xn.astype(BF16)
    bits = lax.bitcast_convert_type(x_hi.astype(F32), jnp.uint32)
    xp_ref[...] = bits[:, :half] | (bits[:, half:] >> 16)

    x_lo = (xn - x_hi.astype(F32)).astype(BF16)
    prod = _dot(jnp.concatenate([x_hi, x_lo], axis=0), wr_ref[...])
    logits = prod[:tm, :LANES] + (prod[:tm, LANES:] + prod[tm:, :LANES])
    lane = lax.broadcasted_iota(jnp.int32, logits.shape, 1)
    lg = jnp.where(lane < N_EXPERTS, logits, -jnp.inf)
    m1 = jnp.max(lg, axis=-1, keepdims=True)
    i1 = jnp.min(jnp.where(lg == m1, lane, LANES), axis=-1, keepdims=True)
    lg2 = jnp.where(lane == i1, -jnp.inf, lg)
    m2 = jnp.max(lg2, axis=-1, keepdims=True)
    i2 = jnp.min(jnp.where(lg2 == m2, lane, LANES), axis=-1, keepdims=True)
    e2 = jnp.exp(m2 - m1)
    den = 1.0 + e2

    onehot = jnp.where(lane == i1, 1.0, jnp.where(lane == i2, 1.0, 0.0))
    rows = lax.broadcasted_iota(jnp.int32, (tm, tm), 0)
    cols = lax.broadcasted_iota(jnp.int32, (tm, tm), 1)
    tri = jnp.where(rows > cols, 1.0, 0.0).astype(BF16)
    pos = _dot(tri, onehot.astype(BF16)) + counts_ref[...]
    rank1 = jnp.sum(jnp.where(lane == i1, pos, 0.0), axis=-1, keepdims=True)
    rank2 = jnp.sum(jnp.where(lane == i2, pos, 0.0), axis=-1, keepdims=True)
    counts_ref[...] += jnp.sum(onehot, axis=0, keepdims=True)

    record = jnp.zeros(logits.shape, F32)
    for lane_id, val in ((R_W1, 1.0 / den), (R_W2, e2 / den), (R_E1, i1.astype(F32)),
                         (R_E2, i2.astype(F32)), (R_RANK1, rank1), (R_RANK2, rank2)):
        record = jnp.where(lane == lane_id, val, record)
    route_ref[...] = record


def _merge(o, gaya, gb, h, wo, wmix, gffn, router=None):
    t = h.shape[0]
    tm = TM_MIX
    row = lambda w: pl.BlockSpec((tm, w), lambda i: (i, 0))
    in_specs = [row(D_MODEL)] * 4 + [_const_spec(wo.shape), _const_spec(wmix.shape),
                                     _const_spec(gffn.shape)]
    args = [o, gaya, gb, h, wo, wmix, gffn]
    if router is None:
        body = _merge_kernel
        out_shape = [jax.ShapeDtypeStruct((t, D_MODEL), F32),
                     jax.ShapeDtypeStruct((t, D_MODEL), BF16)]
        out_specs = [row(D_MODEL), row(D_MODEL)]
    else:
        body = _merge_router_kernel
        in_specs += [_const_spec(router.shape)]
        args += [router]
        out_shape = [jax.ShapeDtypeStruct((t, D_MODEL), F32),
                     jax.ShapeDtypeStruct((t, D_MODEL // 2), jnp.uint32),
                     jax.ShapeDtypeStruct((t, LANES), F32),
                     jax.ShapeDtypeStruct((1, LANES), F32)]
        out_specs = [row(D_MODEL), row(D_MODEL // 2), row(LANES), _const_spec((1, LANES))]
    return pl.pallas_call(
        body, out_shape=out_shape, grid=(t // tm,), in_specs=in_specs, out_specs=out_specs,
        compiler_params=pltpu.CompilerParams(
            dimension_semantics=("arbitrary",), vmem_limit_bytes=VMEM_LIMIT),
        name="merge_router" if router is not None else "merge",
    )(*args)


def _swiglu(x, wg, wu, wd):
    g = _dot(x, wg)
    u = _dot(x, wu)
    return _dot((g * jax.nn.sigmoid(g) * u).astype(BF16), wd)


def _dense_ffn_kernel(x_ref, hmid_ref, wg_ref, wu_ref, wd_ref, out_ref):
    f = pl.program_id(1)

    @pl.when(f == 0)
    def _():
        out_ref[...] = hmid_ref[...]

    out_ref[...] += _swiglu(x_ref[...], wg_ref[...], wu_ref[...], wd_ref[...])


def _dense_ffn(xn, h_mid, wg, wu, wd):
    t = xn.shape[0]
    tm = TM_FFN
    d_ff = wg.shape[1]
    nf = 2
    tf = d_ff // nf
    row = pl.BlockSpec((tm, D_MODEL), lambda i, f: (i, 0))
    return pl.pallas_call(
        _dense_ffn_kernel,
        out_shape=jax.ShapeDtypeStruct((t, D_MODEL), F32),
        grid=(t // tm, nf),
        in_specs=[row, row,
                  pl.BlockSpec((D_MODEL, tf), lambda i, f: (0, f)),
                  pl.BlockSpec((D_MODEL, tf), lambda i, f: (0, f)),
                  pl.BlockSpec((tf, D_MODEL), lambda i, f: (f, 0))],
        out_specs=row,
        compiler_params=pltpu.CompilerParams(
            dimension_semantics=("parallel", "arbitrary"), vmem_limit_bytes=VMEM_LIMIT),
        name="dense_ffn",
    )(xn, h_mid, wg, wu, wd)


def _row_copy(src_ref, src_row, dst_ref, dst_row, sem):
    return pltpu.make_async_copy(src_ref.at[pl.ds(src_row, 1)], dst_ref.at[pl.ds(dst_row, 1)], sem)


def _for_each_row(tm, fn):
    def body(r, carry):
        fn(r)
        return carry
    lax.fori_loop(0, tm, body, 0, unroll=DMA_UNROLL)


def _drain_row_copies(src_ref, dst_ref, sem, n):
    def body(_, carry):
        _row_copy(src_ref, 0, dst_ref, 0, sem).wait()
        return carry
    lax.fori_loop(0, n, body, 0, unroll=DMA_UNROLL)


def _dispatch_kernel(slots_ref, x_ref, _, xs_ref, sem):
    tm = x_ref.shape[0]

    def issue(r):
        for k in range(TOP_K):
            _row_copy(x_ref, r, xs_ref, slots_ref[0, 0, TOP_K * r + k], sem).start()

    _for_each_row(tm, issue)
    _drain_row_copies(x_ref, xs_ref, sem, TOP_K * tm)


def _dispatch(xp, slots, n_rows):
    t, w = xp.shape
    tm = TM_ROUTE
    xs0 = jnp.zeros((n_rows, w), xp.dtype)
    return pl.pallas_call(
        _dispatch_kernel,
        out_shape=jax.ShapeDtypeStruct(xs0.shape, xp.dtype),
        grid=(t // tm,),
        in_specs=[pl.BlockSpec((1, 1, TOP_K * tm), lambda i: (i, 0, 0), memory_space=pltpu.SMEM),
                  pl.BlockSpec((tm, w), lambda i: (i, 0)),
                  pl.BlockSpec(memory_space=pl.ANY)],
        out_specs=pl.BlockSpec(memory_space=pl.ANY),
        scratch_shapes=[pltpu.SemaphoreType.DMA],
        input_output_aliases={2: 0},
        compiler_params=pltpu.CompilerParams(dimension_semantics=("arbitrary",)),
        name="moe_dispatch",
    )(slots, xp, xs0)


def _group_ffn_kernel(te_ref, tb_ref, nv_ref, x_ref, wg_ref, wu_ref, wd_ref, y_ref):
    del te_ref, tb_ref
    live = pl.program_id(0) < nv_ref[0]

    @pl.when(live)
    def _():
        packed = x_ref[...]
        hi = lax.bitcast_convert_type(packed & jnp.uint32(0xFFFF0000), F32)
        lo = lax.bitcast_convert_type(packed << 16, F32)
        x = jnp.concatenate([hi, lo], axis=1).astype(BF16)
        y_ref[...] = _swiglu(x, wg_ref[...], wu_ref[...], wd_ref[...])

    @pl.when(jnp.logical_not(live))
    def _():
        y_ref[...] = jnp.zeros(y_ref.shape, y_ref.dtype)


def _group_ffn(xs, tile_expert, tile_block, n_valid, wg, wu, wd):
    n_rows, w = xs.shape
    tm = TM_GROUP
    _, _, d_ff = wg.shape
    return pl.pallas_call(
        _group_ffn_kernel,
        out_shape=jax.ShapeDtypeStruct((n_rows, D_MODEL), F32),
        grid_spec=pltpu.PrefetchScalarGridSpec(
            num_scalar_prefetch=3,
            grid=(n_rows // tm,),
            in_specs=[pl.BlockSpec((tm, w), lambda i, te, tb, nv: (tb[i], 0)),
                      pl.BlockSpec((None, D_MODEL, d_ff), lambda i, te, tb, nv: (te[i], 0, 0)),
                      pl.BlockSpec((None, D_MODEL, d_ff), lambda i, te, tb, nv: (te[i], 0, 0)),
                      pl.BlockSpec((None, d_ff, D_MODEL), lambda i, te, tb, nv: (te[i], 0, 0))],
            out_specs=pl.BlockSpec((tm, D_MODEL), lambda i, te, tb, nv: (i, 0))),
        compiler_params=pltpu.CompilerParams(
            dimension_semantics=("arbitrary",), vmem_limit_bytes=VMEM_LIMIT),
        name="moe_group_ffn",
    )(tile_expert, tile_block, n_valid, xs, wg, wu, wd)


def _combine_kernel(final_norm, slots_ref, route_ref, hmid_ref, gfin_ref, ys_ref, out_ref,
                    ybuf, sem):
    tm = hmid_ref.shape[0]

    def issue(r):
        for k in range(TOP_K):
            _row_copy(ys_ref, slots_ref[0, 0, TOP_K * r + k], ybuf.at[k], r, sem).start()

    _for_each_row(tm, issue)
    _drain_row_copies(ys_ref, ybuf.at[0], sem, TOP_K * tm)
    route = route_ref[...]
    h = (hmid_ref[...] + route[:, R_W1:R_W1 + 1] * ybuf[0]) + route[:, R_W2:R_W2 + 1] * ybuf[1]
    out_ref[...] = _rms(h, gfin_ref[...]) if final_norm else h


def _combine(ys, slots, route, h_mid, gfin, final_norm):
    t = h_mid.shape[0]
    tm = TM_ROUTE
    row = lambda w: pl.BlockSpec((tm, w), lambda i: (i, 0))
    return pl.pallas_call(
        functools.partial(_combine_kernel, final_norm),
        out_shape=jax.ShapeDtypeStruct((t, D_MODEL), F32),
        grid=(t // tm,),
        in_specs=[pl.BlockSpec((1, 1, TOP_K * tm), lambda i: (i, 0, 0), memory_space=pltpu.SMEM),
                  row(LANES), row(D_MODEL), _const_spec(gfin.shape),
                  pl.BlockSpec(memory_space=pl.ANY)],
        out_specs=row(D_MODEL),
        scratch_shapes=[pltpu.VMEM((TOP_K, tm, D_MODEL), F32), pltpu.SemaphoreType.DMA],
        compiler_params=pltpu.CompilerParams(dimension_semantics=("arbitrary",)),
        name="moe_combine",
    )(slots, route, h_mid, gfin, ys)


def _moe_ffn(xp, route, counts, h_mid, wg, wu, wd, gfin, final_norm):
    t = h_mid.shape[0]
    tm = TM_GROUP
    n_tiles = TOP_K * t // tm + N_EXPERTS
    e1 = route[:, R_E1].astype(jnp.int32)
    e2 = route[:, R_E2].astype(jnp.int32)
    rank1 = route[:, R_RANK1].astype(jnp.int32)
    rank2 = route[:, R_RANK2].astype(jnp.int32)
    tiles_per_expert = (counts[0, :N_EXPERTS].astype(jnp.int32) + tm - 1) // tm
    tile_end = jnp.cumsum(tiles_per_expert)
    row_start = (tile_end - tiles_per_expert) * tm
    slots = jnp.stack([row_start[e1] + rank1, row_start[e2] + rank2], axis=1)
    slots = slots.reshape(t // TM_ROUTE, 1, TOP_K * TM_ROUTE)
    n_valid = tile_end[-1:]
    tile_block = jnp.minimum(jnp.arange(n_tiles, dtype=jnp.int32), n_valid - 1)
    tile_expert = jnp.sum((tile_block[:, None] >= tile_end[None, :]).astype(jnp.int32), axis=1)
    tile_expert = jnp.minimum(tile_expert, N_EXPERTS - 1)
    xs = _dispatch(xp, slots, n_tiles * tm)
    ys = _group_ffn(xs, tile_expert, tile_block, n_valid, wg, wu, wd)
    return _combine(ys, slots, route, h_mid, gfin, final_norm)


def _final_norm_kernel(h_ref, g_ref, out_ref):
    out_ref[...] = _rms(h_ref[...], g_ref[...])


def _final_norm(h, g):
    t = h.shape[0]
    row = pl.BlockSpec((TM_FFN, D_MODEL), lambda i: (i, 0))
    return pl.pallas_call(
        _final_norm_kernel, out_shape=jax.ShapeDtypeStruct((t, D_MODEL), F32),
        grid=(t // TM_FFN,), in_specs=[row, _const_spec(g.shape)], out_specs=row,
        name="final_norm",
    )(h, g)


def _pack_w_in(w):
    d = w.shape[0]
    kr0 = 3 * CONV_WIDTH + Q_LORA + KV_LORA
    kr_a = w[:, kr0:kr0 + HALF_ROPE]
    kr_b = w[:, kr0 + HALF_ROPE:kr0 + QK_ROPE]
    z_lo = jnp.zeros((d, QK_NOPE), w.dtype)
    z_hi = jnp.zeros((d, HEAD_PAD - QK_NOPE - QK_ROPE), w.dtype)
    return jnp.concatenate(
        [w[:, :kr0], z_lo, kr_a, kr_b, z_hi, z_lo, kr_b, kr_a, z_hi, w[:, kr0 + QK_ROPE:]], axis=1)


def _pack_w_uq(w):
    w3 = w.reshape(Q_LORA, MLA_HEADS, QK_NOPE + QK_ROPE)
    nope = w3[..., :QK_NOPE]
    r_a = w3[..., QK_NOPE:QK_NOPE + HALF_ROPE]
    r_b = w3[..., QK_NOPE + HALF_ROPE:]
    pad = jnp.zeros((Q_LORA, MLA_HEADS, HEAD_PAD - QK_NOPE - QK_ROPE), w.dtype)
    plain = jnp.concatenate([nope, r_a, r_b, pad], axis=-1)
    swapped = jnp.concatenate([jnp.zeros_like(nope), r_b, r_a, pad], axis=-1)
    return (plain.reshape(Q_LORA, MLA_HEADS * HEAD_PAD),
            swapped.reshape(Q_LORA, MLA_HEADS * HEAD_PAD))


def _pack_w_ukv(w):
    w3 = w.reshape(KV_LORA, MLA_HEADS, QK_NOPE + V_DIM)
    k = jnp.concatenate([w3[..., :QK_NOPE],
                         jnp.zeros((KV_LORA, MLA_HEADS, HEAD_PAD - QK_NOPE), w.dtype)], axis=-1)
    return (k.reshape(KV_LORA, MLA_HEADS * HEAD_PAD),
            w3[..., QK_NOPE:].reshape(KV_LORA, MLA_HEADS * V_DIM))


def kernel(x, positions, norm_mix, w_in, b_gate, conv_w, w_conv_out, norm_q, norm_kv, w_uq, w_ukv,
           w_attn_out, w_mix_out, norm_ffn, dense_w_gate, dense_w_up, dense_w_down, router_w,
           moe_w_gate, moe_w_up, moe_w_down, norm_final):
    batch, seq, d = x.shape
    t = batch * seq
    c_tab, s_tab = _rope_tables(positions)
    h = x.reshape(t, d)
    for layer in range(DEPTH):
        wuq, wuqs = _pack_w_uq(w_uq[layer])
        wuk, wuv = _pack_w_ukv(w_ukv[layer])
        q, k, v, gaya, gb = _front(
            h, seq, norm_mix[layer].reshape(1, d), _pack_w_in(w_in[layer]).astype(BF16),
            b_gate[layer].reshape(1, 2 * d), conv_w[layer].reshape(CONV_K, CONV_WIDTH),
            w_conv_out[layer].astype(BF16), norm_q[layer].reshape(1, Q_LORA),
            norm_kv[layer].reshape(1, KV_LORA), wuq.astype(BF16), wuqs.astype(BF16),
            wuk.astype(BF16), wuv.astype(BF16), c_tab, s_tab)
        o = _attention(q, k, v, batch, seq)
        i = layer // 2
        last = layer == DEPTH - 1
        merge_args = (o, gaya, gb, h, w_attn_out[layer].astype(BF16),
                      w_mix_out[layer].astype(BF16), norm_ffn[layer].reshape(1, d))
        if layer % 2 == 0:
            h_mid, xn = _merge(*merge_args)
            h = _dense_ffn(xn, h_mid, dense_w_gate[i].astype(BF16), dense_w_up[i].astype(BF16),
                           dense_w_down[i].astype(BF16))
            if last:
                h = _final_norm(h, norm_final.reshape(1, d))
        else:
            wr = jnp.pad(router_w[i], ((0, 0), (0, LANES - N_EXPERTS)))
            wr_hi = wr.astype(BF16)
            wr_lo = (wr - wr_hi.astype(F32)).astype(BF16)
            h_mid, xp, route, counts = _merge(*merge_args,
                                              router=jnp.concatenate([wr_hi, wr_lo], axis=1))
            h = _moe_ffn(xp, route, counts, h_mid, moe_w_gate[i].astype(BF16),
                         moe_w_up[i].astype(BF16), moe_w_down[i].astype(BF16),
                         norm_final.reshape(1, d), last)
    return h.reshape(batch, seq, d)
```

```python
import functools

import jax
import jax.numpy as jnp
from jax import lax
from jax.experimental import pallas as pl
from jax.experimental.pallas import tpu as pltpu

F32 = jnp.float32
BF16 = jnp.bfloat16

D_MODEL = 1024
DEPTH = 2
CHUNK = 64
CONV_WIDTH = 1024
CONV_K = 3
MLA_HEADS = 16
QK_NOPE = 64
QK_ROPE = 32
V_DIM = 64
Q_LORA = 256
KV_LORA = 128
ROPE_THETA = 10000.0
N_EXPERTS = 8
TOP_K = 2
EPS = 1e-6
LOG2_E = 1.4426950408889634

LANES = 128
SUBLANES = 8
HEAD_PAD = LANES
HALF_ROPE = QK_ROPE // 2

C_GATE_B = 0
C_GATE_C = C_GATE_B + CONV_WIDTH
C_U = C_GATE_C + CONV_WIDTH
C_Q = C_U + CONV_WIDTH
C_KV = C_Q + Q_LORA
C_KROPE = C_KV + KV_LORA

TM_FRONT = 512
TM_MIX = 512
TM_FFN = 512
TM_ROUTE = 512
TM_GROUP = 512
DMA_UNROLL = 8
TQ = 512
HEADS_PER_STEP = 4
VMEM_LIMIT = 48 * 1024 * 1024


def _rms(x, g):
    inv = lax.rsqrt(jnp.mean(x * x, axis=-1, keepdims=True) + EPS)
    return (x * inv) * g


def _dot(a, b):
    return jnp.dot(a, b, preferred_element_type=F32)


def _const_spec(shape, single_buffer=False):
    nd = len(shape)
    mode = pl.Buffered(1) if single_buffer else None
    return pl.BlockSpec(shape, lambda *_: (0,) * nd, pipeline_mode=mode)


def _rope_kernel(pos_ref, invf_ref, cos_ref, sin_ref, nsin_ref):
    ang = pos_ref[...].astype(F32) * invf_ref[...]
    c = jnp.cos(ang)
    s = jnp.sin(ang)
    cos_ref[...] = c
    sin_ref[...] = s
    nsin_ref[...] = -s


def _rope_tables(positions):
    t = positions.size
    rows = t * HALF_ROPE // LANES
    pos_rep = jnp.repeat(positions.reshape(t), HALF_ROPE).reshape(rows, LANES)
    inv_freq = ROPE_THETA ** (-jnp.arange(0, QK_ROPE, 2, dtype=F32) / QK_ROPE)
    invf = jnp.tile(inv_freq, LANES // HALF_ROPE).reshape(1, LANES)
    blk = 512
    spec = pl.BlockSpec((blk, LANES), lambda i: (i, 0))
    cos, sin, nsin = pl.pallas_call(
        _rope_kernel,
        out_shape=[jax.ShapeDtypeStruct((rows, LANES), F32)] * 3,
        grid=(rows // blk,),
        in_specs=[spec, _const_spec((1, LANES))],
        out_specs=[spec] * 3,
        name="rope_tables",
    )(pos_rep, invf)
    cos = cos.reshape(t, HALF_ROPE)
    sin = sin.reshape(t, HALF_ROPE)
    nsin = nsin.reshape(t, HALF_ROPE)
    ones = jnp.ones((t, QK_NOPE), F32)
    zeros = jnp.zeros((t, QK_NOPE), F32)
    pad = jnp.zeros((t, HEAD_PAD - QK_NOPE - QK_ROPE), F32)
    c_tab = jnp.concatenate([ones, cos, cos, pad], axis=1)
    s_tab = jnp.concatenate([zeros, nsin, sin, pad], axis=1)
    return c_tab, s_tab


def _front_kernel(tiles_per_seq, x_ref, gmix_ref, win_ref, wkr_ref, wgl_ref, bgate_ref, convw_ref, wco_ref,
                  gq_ref, gkv_ref, wuq_ref, wuqs_ref, wuk_ref, wuv_ref, c_ref, s_ref,
                  q_ref, k_ref, v_ref, gaya_ref, gb_ref, zbuf):
    tm = x_ref.shape[0]
    i = pl.program_id(0)
    xn = _rms(x_ref[...], gmix_ref[...]).astype(BF16)

    def proj(lo, width):
        return _dot(xn, win_ref[:, lo:lo + width])

    z = proj(C_GATE_C, CONV_WIDTH) * proj(C_U, CONV_WIDTH)

    @pl.when(i % tiles_per_seq == 0)
    def _():
        zbuf[0:SUBLANES, :] = jnp.zeros((SUBLANES, CONV_WIDTH), F32)

    zbuf[SUBLANES:SUBLANES + tm, :] = z
    cw = convw_ref[...]
    conv = (cw[2:3, :] * z
            + cw[1:2, :] * zbuf[SUBLANES - 1:SUBLANES - 1 + tm, :]
            + cw[0:1, :] * zbuf[SUBLANES - 2:SUBLANES - 2 + tm, :])
    zbuf[0:SUBLANES, :] = zbuf[tm:tm + SUBLANES, :]
    y_a = _dot((proj(C_GATE_B, CONV_WIDTH) * conv).astype(BF16), wco_ref[...])

    gate = jax.nn.sigmoid(_dot(xn, wgl_ref[...]) + bgate_ref[...])
    gaya_ref[...] = (gate[:, :D_MODEL] * y_a).astype(BF16)
    gb_ref[...] = gate[:, D_MODEL:].astype(BF16)

    c_tab = c_ref[...]
    s_tab = s_ref[...]
    c_all = jnp.tile(c_tab, (1, MLA_HEADS))
    s_all = jnp.tile(s_tab, (1, MLA_HEADS))
    cqn = _rms(proj(C_Q, Q_LORA), gq_ref[...]).astype(BF16)
    q = _dot(cqn, wuq_ref[...]) * c_all + _dot(cqn, wuqs_ref[...]) * s_all
    q_ref[...] = (q * (LOG2_E * (QK_NOPE + QK_ROPE) ** -0.5)).astype(BF16)

    ckvn = _rms(proj(C_KV, KV_LORA), gkv_ref[...]).astype(BF16)
    k_rot = _dot(xn, wkr_ref[:, :LANES]) * c_tab + _dot(xn, wkr_ref[:, LANES:]) * s_tab
    k_ref[...] = (_dot(ckvn, wuk_ref[...]) + jnp.tile(k_rot, (1, MLA_HEADS))).astype(BF16)
    v_ref[...] = _dot(ckvn, wuv_ref[...]).astype(BF16)


def _front(h, seq, gmix, win, wkr, wgl, bgate, convw, wco, gq, gkv, wuq, wuqs, wuk, wuv,
           c_tab, s_tab):
    t = h.shape[0]
    tm = TM_FRONT
    hp = MLA_HEADS * HEAD_PAD
    row = lambda w: pl.BlockSpec((tm, w), lambda i: (i, 0))
    return pl.pallas_call(
        functools.partial(_front_kernel, seq // tm),
        out_shape=[jax.ShapeDtypeStruct((t, hp), BF16),
                   jax.ShapeDtypeStruct((t, hp), BF16),
                   jax.ShapeDtypeStruct((t, MLA_HEADS * V_DIM), BF16),
                   jax.ShapeDtypeStruct((t, D_MODEL), BF16),
                   jax.ShapeDtypeStruct((t, D_MODEL), BF16)],
        grid=(t // tm,),
        in_specs=[row(D_MODEL)]
        + [_const_spec(a.shape, single_buffer=True)
           for a in (gmix, win, wkr, wgl, bgate, convw, wco, gq, gkv, wuq, wuqs, wuk, wuv)]
        + [row(LANES), row(LANES)],
        out_specs=[row(hp), row(hp), row(MLA_HEADS * V_DIM), row(D_MODEL), row(D_MODEL)],
        scratch_shapes=[pltpu.VMEM((tm + SUBLANES, CONV_WIDTH), F32)],
        compiler_params=pltpu.CompilerParams(
            dimension_semantics=("arbitrary",), vmem_limit_bytes=VMEM_LIMIT),
        name="mixer_front",
    )(h, gmix, win, wkr, wgl, bgate, convw, wco, gq, gkv, wuq, wuqs, wuk, wuv, c_tab, s_tab)


def _attn_kernel(tq, n_cast, q_ref, k_ref, v_ref, *rest):
    o_ref = rest[n_cast]
    for src, dst in zip(rest[:n_cast], rest[n_cast + 1:]):
        dst[...] = src[...].astype(dst.dtype)
    seq = q_ref.shape[0]
    n_heads = q_ref.shape[1] // HEAD_PAD
    row_chunk = lax.broadcasted_iota(jnp.int32, (tq, tq), 0) // CHUNK
    col_chunk = lax.broadcasted_iota(jnp.int32, (tq, tq), 1) // CHUNK
    diag_mask = col_chunk <= row_chunk
    v_lane = lax.broadcasted_iota(jnp.int32, (seq, 2 * V_DIM), 1)
    contract_last = (((1,), (1,)), ((), ()))

    for hh in range(n_heads):
        cols = slice(hh * HEAD_PAD, (hh + 1) * HEAD_PAD)
        o_cols = slice(hh * V_DIM, (hh + 1) * V_DIM)
        odd = hh % 2
        v_pair = v_ref[:, (hh - odd) * V_DIM:(hh - odd + 2) * V_DIM]
        own = (v_lane >= V_DIM) if odd else (v_lane < V_DIM)
        v_ext = jnp.where(own, v_pair, jnp.ones((), v_pair.dtype))
        for qi in reversed(range(seq // tq)):
            q_lo = qi * tq
            q = q_ref[q_lo:q_lo + tq, cols]
            s_diag = lax.dot_general(q, k_ref[q_lo:q_lo + tq, cols], contract_last,
                                     preferred_element_type=F32)
            s_diag = jnp.where(diag_mask, s_diag, -jnp.inf)
            m = jnp.max(s_diag, axis=-1, keepdims=True)
            if qi:
                s_past = lax.dot_general(q, k_ref[0:q_lo, cols], contract_last,
                                         preferred_element_type=F32)
                m = jnp.maximum(m, jnp.max(s_past, axis=-1, keepdims=True))
            acc = _dot(jnp.exp2(s_diag - m).astype(BF16), v_ext[q_lo:q_lo + tq, :])
            if qi:
                acc = acc + _dot(jnp.exp2(s_past - m).astype(BF16), v_ext[0:q_lo, :])
            out = (acc / pltpu.roll(acc, V_DIM, axis=1)).astype(o_ref.dtype)
            o_ref[q_lo:q_lo + tq, o_cols] = out[:, odd * V_DIM:(odd + 1) * V_DIM]


def _attention(q, k, v, batch, seq, to_cast=()):
    t = q.shape[0]
    hps = HEADS_PER_STEP
    groups = MLA_HEADS // hps
    steps = batch * groups
    slabs = [w.reshape(-1, w.shape[-1]) for w in to_cast]
    slab_specs = [pl.BlockSpec((s.shape[0] // steps, s.shape[1]), lambda b, p: (b * groups + p, 0))
                  for s in slabs]
    outs = pl.pallas_call(
        functools.partial(_attn_kernel, TQ, len(slabs)),
        out_shape=[jax.ShapeDtypeStruct((t, MLA_HEADS * V_DIM), BF16)]
        + [jax.ShapeDtypeStruct(s.shape, BF16) for s in slabs],
        grid=(batch, groups),
        in_specs=[pl.BlockSpec((seq, hps * HEAD_PAD), lambda b, p: (b, p)),
                  pl.BlockSpec((seq, hps * HEAD_PAD), lambda b, p: (b, p)),
                  pl.BlockSpec((seq, hps * V_DIM), lambda b, p: (b, p))] + slab_specs,
        out_specs=[pl.BlockSpec((seq, hps * V_DIM), lambda b, p: (b, p))] + slab_specs,
        compiler_params=pltpu.CompilerParams(
            dimension_semantics=("parallel", "parallel"), vmem_limit_bytes=VMEM_LIMIT),
        name="chunk_attention",
    )(q, k, v, *slabs)
    return outs[0], [o.reshape(w.shape) for o, w in zip(outs[1:], to_cast)]


def _merge_body(o_ref, gaya_ref, gb_ref, h_ref, wo_ref, wmix_ref, gffn_ref, hmid_ref):
    y_b = _dot(o_ref[...], wo_ref[...])
    merged = gaya_ref[...].astype(F32) + gb_ref[...].astype(F32) * y_b
    h_mid = h_ref[...] + _dot(merged.astype(BF16), wmix_ref[...])
    hmid_ref[...] = h_mid
    return _rms(h_mid, gffn_ref[...])


def _merge_kernel(o_ref, gaya_ref, gb_ref, h_ref, wo_ref, wmix_ref, gffn_ref, hmid_ref, xn_ref):
    xn = _merge_body(o_ref, gaya_ref, gb_ref, h_ref, wo_ref, wmix_ref, gffn_ref, hmid_ref)
    xn_ref[...] = xn.astype(BF16)


R_W1, R_W2, R_E1, R_E2, R_RANK1, R_RANK2 = range(6)


def _merge_router_kernel(o_ref, gaya_ref, gb_ref, h_ref, wo_ref, wmix_ref, gffn_ref,
                         wr_ref, hmid_ref, xp_ref, route_ref, route_t_ref, counts_ref):
    xn = _merge_body(o_ref, gaya_ref, gb_ref, h_ref, wo_ref, wmix_ref, gffn_ref, hmid_ref)
    tm = xn.shape[0]

    @pl.when(pl.program_id(0) == 0)
    def _():
        counts_ref[...] = jnp.zeros(counts_ref.shape, F32)

    x_hi = xn.astype(BF16)
    xp_ref[...] = x_hi.astype(F32)

    x_lo = (xn - x_hi.astype(F32)).astype(BF16)
    prod = _dot(jnp.concatenate([x_hi, x_lo], axis=0), wr_ref[...])
    logits = prod[:tm, :LANES] + (prod[:tm, LANES:] + prod[tm:, :LANES])
    lane = lax.broadcasted_iota(jnp.int32, logits.shape, 1)
    lg = jnp.where(lane < N_EXPERTS, logits, -jnp.inf)
    m1 = jnp.max(lg, axis=-1, keepdims=True)
    i1 = jnp.min(jnp.where(lg == m1, lane, LANES), axis=-1, keepdims=True)
    lg2 = jnp.where(lane == i1, -jnp.inf, lg)
    m2 = jnp.max(lg2, axis=-1, keepdims=True)
    i2 = jnp.min(jnp.where(lg2 == m2, lane, LANES), axis=-1, keepdims=True)
    e2 = jnp.exp(m2 - m1)
    den = 1.0 + e2

    onehot = jnp.where(lane == i1, 1.0, jnp.where(lane == i2, 1.0, 0.0))
    rows = lax.broadcasted_iota(jnp.int32, (tm, tm), 0)
    cols = lax.broadcasted_iota(jnp.int32, (tm, tm), 1)
    tri = jnp.where(rows > cols, 1.0, 0.0).astype(BF16)
    pos = _dot(tri, onehot.astype(BF16)) + counts_ref[...]
    rank1 = jnp.sum(jnp.where(lane == i1, pos, 0.0), axis=-1, keepdims=True)
    rank2 = jnp.sum(jnp.where(lane == i2, pos, 0.0), axis=-1, keepdims=True)
    counts_ref[...] += jnp.sum(onehot, axis=0, keepdims=True)

    record = jnp.zeros(logits.shape, F32)
    for lane_id, val in ((R_W1, 1.0 / den), (R_W2, e2 / den), (R_E1, i1.astype(F32)),
                         (R_E2, i2.astype(F32)), (R_RANK1, rank1), (R_RANK2, rank2)):
        record = jnp.where(lane == lane_id, val, record)
    route_ref[...] = record
    route_t_ref[...] = jnp.transpose(record)[:SUBLANES, :]


def _merge(o, gaya, gb, h, wo, wmix, gffn, router=None):
    t = h.shape[0]
    tm = TM_MIX
    row = lambda w: pl.BlockSpec((tm, w), lambda i: (i, 0))
    in_specs = [row(D_MODEL)] * 4 + [_const_spec(wo.shape), _const_spec(wmix.shape),
                                     _const_spec(gffn.shape)]
    args = [o, gaya, gb, h, wo, wmix, gffn]
    if router is None:
        body = _merge_kernel
        out_shape = [jax.ShapeDtypeStruct((t, D_MODEL), F32),
                     jax.ShapeDtypeStruct((t, D_MODEL), BF16)]
        out_specs = [row(D_MODEL), row(D_MODEL)]
    else:
        body = _merge_router_kernel
        in_specs += [_const_spec(router.shape)]
        args += [router]
        out_shape = [jax.ShapeDtypeStruct((t, D_MODEL), F32),
                     jax.ShapeDtypeStruct((t, D_MODEL), F32),
                     jax.ShapeDtypeStruct((t, LANES), F32),
                     jax.ShapeDtypeStruct((SUBLANES, t), F32),
                     jax.ShapeDtypeStruct((1, LANES), F32)]
        out_specs = [row(D_MODEL), row(D_MODEL), row(LANES),
                     pl.BlockSpec((SUBLANES, tm), lambda i: (0, i)), _const_spec((1, LANES))]
    return pl.pallas_call(
        body, out_shape=out_shape, grid=(t // tm,), in_specs=in_specs, out_specs=out_specs,
        compiler_params=pltpu.CompilerParams(
            dimension_semantics=("arbitrary",), vmem_limit_bytes=VMEM_LIMIT),
        name="merge_router" if router is not None else "merge",
    )(*args)


def _swiglu(x, wg, wu, wd):
    g = _dot(x, wg)
    u = _dot(x, wu)
    return _dot((g * jax.nn.sigmoid(g) * u).astype(BF16), wd)


def _dense_ffn_kernel(x_ref, hmid_ref, wg_ref, wu_ref, wd_ref, out_ref):
    out_ref[...] = hmid_ref[...] + _swiglu(x_ref[...], wg_ref[...], wu_ref[...], wd_ref[...])


def _dense_ffn(xn, h_mid, wg, wu, wd):
    t = xn.shape[0]
    tm = TM_FFN
    row = pl.BlockSpec((tm, D_MODEL), lambda i: (i, 0))
    return pl.pallas_call(
        _dense_ffn_kernel,
        out_shape=jax.ShapeDtypeStruct((t, D_MODEL), F32),
        grid=(t // tm,),
        in_specs=[row, row] + [_const_spec(w.shape, single_buffer=True) for w in (wg, wu, wd)],
        out_specs=row,
        compiler_params=pltpu.CompilerParams(
            dimension_semantics=("parallel",), vmem_limit_bytes=VMEM_LIMIT),
        name="dense_ffn",
    )(xn, h_mid, wg, wu, wd)


def _row_copy(src_ref, src_row, dst_ref, dst_row, sem):
    return pltpu.make_async_copy(src_ref.at[pl.ds(src_row, 1)], dst_ref.at[pl.ds(dst_row, 1)], sem)


def _for_each_row(tm, fn):
    def body(r, carry):
        fn(r)
        return carry
    lax.fori_loop(0, tm, body, 0, unroll=DMA_UNROLL)


def _drain_row_copies(src_ref, dst_ref, sem, n):
    def body(_, carry):
        _row_copy(src_ref, 0, dst_ref, 0, sem).wait()
        return carry
    lax.fori_loop(0, n, body, 0, unroll=DMA_UNROLL)


def _dispatch_kernel(slots_ref, x_ref, _, xs_ref, sem):
    tm = x_ref.shape[0]

    def issue(r):
        for k in range(TOP_K):
            _row_copy(x_ref, r, xs_ref, slots_ref[0, 0, TOP_K * r + k], sem).start()

    _for_each_row(tm, issue)
    _drain_row_copies(x_ref, xs_ref, sem, TOP_K * tm)


def _dispatch(xp, slots, n_rows):
    t, w = xp.shape
    tm = TM_ROUTE
    xs0 = jnp.zeros((n_rows, w), xp.dtype)
    return pl.pallas_call(
        _dispatch_kernel,
        out_shape=jax.ShapeDtypeStruct(xs0.shape, xp.dtype),
        grid=(t // tm,),
        in_specs=[pl.BlockSpec((1, 1, TOP_K * tm), lambda i: (i, 0, 0), memory_space=pltpu.SMEM),
                  pl.BlockSpec((tm, w), lambda i: (i, 0)),
                  pl.BlockSpec(memory_space=pl.ANY)],
        out_specs=pl.BlockSpec(memory_space=pl.ANY),
        scratch_shapes=[pltpu.SemaphoreType.DMA],
        input_output_aliases={2: 0},
        compiler_params=pltpu.CompilerParams(dimension_semantics=("arbitrary",)),
        name="moe_dispatch",
    )(slots, xp, xs0)


def _group_ffn_kernel(te_ref, tb_ref, nv_ref, x_ref, wg_ref, wu_ref, wd_ref, y_ref):
    del te_ref, tb_ref
    live = pl.program_id(0) < nv_ref[0]

    @pl.when(live)
    def _():
        y_ref[...] = _swiglu(x_ref[...].astype(BF16), wg_ref[...], wu_ref[...], wd_ref[...])

    @pl.when(jnp.logical_not(live))
    def _():
        y_ref[...] = jnp.zeros(y_ref.shape, y_ref.dtype)


def _group_ffn(xs, tile_expert, tile_block, n_valid, wg, wu, wd):
    n_rows, w = xs.shape
    tm = TM_GROUP
    _, _, d_ff = wg.shape
    return pl.pallas_call(
        _group_ffn_kernel,
        out_shape=jax.ShapeDtypeStruct((n_rows, D_MODEL), F32),
        grid_spec=pltpu.PrefetchScalarGridSpec(
            num_scalar_prefetch=3,
            grid=(n_rows // tm,),
            in_specs=[pl.BlockSpec((tm, w), lambda i, te, tb, nv: (tb[i], 0)),
                      pl.BlockSpec((None, D_MODEL, d_ff), lambda i, te, tb, nv: (te[i], 0, 0)),
                      pl.BlockSpec((None, D_MODEL, d_ff), lambda i, te, tb, nv: (te[i], 0, 0)),
                      pl.BlockSpec((None, d_ff, D_MODEL), lambda i, te, tb, nv: (te[i], 0, 0))],
            out_specs=pl.BlockSpec((tm, D_MODEL), lambda i, te, tb, nv: (i, 0))),
        compiler_params=pltpu.CompilerParams(
            dimension_semantics=("arbitrary",), vmem_limit_bytes=VMEM_LIMIT),
        name="moe_group_ffn",
    )(tile_expert, tile_block, n_valid, xs, wg, wu, wd)


def _combine_kernel(final_norm, slots_ref, route_ref, hmid_ref, gfin_ref, ys_ref, out_ref,
                    ybuf, sem):
    tm = hmid_ref.shape[0]

    def issue(r):
        for k in range(TOP_K):
            _row_copy(ys_ref, slots_ref[0, 0, TOP_K * r + k], ybuf.at[k], r, sem).start()

    _for_each_row(tm, issue)
    _drain_row_copies(ys_ref, ybuf.at[0], sem, TOP_K * tm)
    route = route_ref[...]
    h = (hmid_ref[...] + route[:, R_W1:R_W1 + 1] * ybuf[0]) + route[:, R_W2:R_W2 + 1] * ybuf[1]
    out_ref[...] = _rms(h, gfin_ref[...]) if final_norm else h


def _combine(ys, slots, route, h_mid, gfin, final_norm):
    t = h_mid.shape[0]
    tm = TM_ROUTE
    row = lambda w: pl.BlockSpec((tm, w), lambda i: (i, 0))
    return pl.pallas_call(
        functools.partial(_combine_kernel, final_norm),
        out_shape=jax.ShapeDtypeStruct((t, D_MODEL), F32),
        grid=(t // tm,),
        in_specs=[pl.BlockSpec((1, 1, TOP_K * tm), lambda i: (i, 0, 0), memory_space=pltpu.SMEM),
                  row(LANES), row(D_MODEL), _const_spec(gfin.shape),
                  pl.BlockSpec(memory_space=pl.ANY)],
        out_specs=row(D_MODEL),
        scratch_shapes=[pltpu.VMEM((TOP_K, tm, D_MODEL), F32), pltpu.SemaphoreType.DMA],
        compiler_params=pltpu.CompilerParams(dimension_semantics=("arbitrary",)),
        name="moe_combine",
    )(slots, route, h_mid, gfin, ys)


def _moe_ffn(xp, route, route_t, counts, h_mid, wg, wu, wd, gfin, final_norm):
    t = h_mid.shape[0]
    tm = TM_GROUP
    n_tiles = TOP_K * t // tm + N_EXPERTS
    tiles_per_expert = (counts[0, :N_EXPERTS].astype(jnp.int32) + tm - 1) // tm
    tile_end = jnp.cumsum(tiles_per_expert)
    row_start = (tile_end - tiles_per_expert) * tm
    experts = jnp.arange(N_EXPERTS, dtype=jnp.int32)

    def slot(expert_row, rank_row):
        expert = route_t[expert_row].astype(jnp.int32)
        start = jnp.sum(jnp.where(expert[:, None] == experts[None, :], row_start[None, :], 0), axis=1)
        return start + route_t[rank_row].astype(jnp.int32)

    slots = jnp.stack([slot(R_E1, R_RANK1), slot(R_E2, R_RANK2)], axis=1)
    slots = slots.reshape(t // TM_ROUTE, 1, TOP_K * TM_ROUTE)
    n_valid = tile_end[-1:]
    tile_block = jnp.minimum(jnp.arange(n_tiles, dtype=jnp.int32), n_valid - 1)
    tile_expert = jnp.sum((tile_block[:, None] >= tile_end[None, :]).astype(jnp.int32), axis=1)
    tile_expert = jnp.minimum(tile_expert, N_EXPERTS - 1)
    xs = _dispatch(xp, slots, n_tiles * tm)
    ys = _group_ffn(xs, tile_expert, tile_block, n_valid, wg, wu, wd)
    return _combine(ys, slots, route, h_mid, gfin, final_norm)


def _final_norm_kernel(h_ref, g_ref, out_ref):
    out_ref[...] = _rms(h_ref[...], g_ref[...])


def _final_norm(h, g):
    t = h.shape[0]
    row = pl.BlockSpec((TM_FFN, D_MODEL), lambda i: (i, 0))
    return pl.pallas_call(
        _final_norm_kernel, out_shape=jax.ShapeDtypeStruct((t, D_MODEL), F32),
        grid=(t // TM_FFN,), in_specs=[row, _const_spec(g.shape)], out_specs=row,
        name="final_norm",
    )(h, g)


def _pack_w_in(w):
    d = w.shape[0]
    kr_a = w[:, C_KROPE:C_KROPE + HALF_ROPE]
    kr_b = w[:, C_KROPE + HALF_ROPE:C_KROPE + QK_ROPE]
    z_lo = jnp.zeros((d, QK_NOPE), w.dtype)
    z_hi = jnp.zeros((d, HEAD_PAD - QK_NOPE - QK_ROPE), w.dtype)
    w_kr = jnp.concatenate([z_lo, kr_a, kr_b, z_hi, z_lo, kr_b, kr_a, z_hi], axis=1)
    return (w[:, :C_KROPE].astype(BF16), w_kr.astype(BF16), w[:, C_KROPE + QK_ROPE:].astype(BF16))


def _pack_w_uq(w):
    w3 = w.reshape(Q_LORA, MLA_HEADS, QK_NOPE + QK_ROPE)
    nope = w3[..., :QK_NOPE]
    r_a = w3[..., QK_NOPE:QK_NOPE + HALF_ROPE]
    r_b = w3[..., QK_NOPE + HALF_ROPE:]
    pad = jnp.zeros((Q_LORA, MLA_HEADS, HEAD_PAD - QK_NOPE - QK_ROPE), w.dtype)
    plain = jnp.concatenate([nope, r_a, r_b, pad], axis=-1)
    swapped = jnp.concatenate([jnp.zeros_like(nope), r_b, r_a, pad], axis=-1)
    return (plain.reshape(Q_LORA, MLA_HEADS * HEAD_PAD),
            swapped.reshape(Q_LORA, MLA_HEADS * HEAD_PAD))


def _pack_w_ukv(w):
    w3 = w.reshape(KV_LORA, MLA_HEADS, QK_NOPE + V_DIM)
    k = jnp.concatenate([w3[..., :QK_NOPE],
                         jnp.zeros((KV_LORA, MLA_HEADS, HEAD_PAD - QK_NOPE), w.dtype)], axis=-1)
    return (k.reshape(KV_LORA, MLA_HEADS * HEAD_PAD),
            w3[..., QK_NOPE:].reshape(KV_LORA, MLA_HEADS * V_DIM))


def kernel(x, positions, norm_mix, w_in, b_gate, conv_w, w_conv_out, norm_q, norm_kv, w_uq, w_ukv,
           w_attn_out, w_mix_out, norm_ffn, dense_w_gate, dense_w_up, dense_w_down, router_w,
           moe_w_gate, moe_w_up, moe_w_down, norm_final):
    batch, seq, d = x.shape
    t = batch * seq
    c_tab, s_tab = _rope_tables(positions)
    h = x.reshape(t, d)
    expert_w = None
    for layer in range(DEPTH):
        wuq, wuqs = _pack_w_uq(w_uq[layer])
        wuk, wuv = _pack_w_ukv(w_ukv[layer])
        q, k, v, gaya, gb = _front(
            h, seq, norm_mix[layer].reshape(1, d), *_pack_w_in(w_in[layer]),
            b_gate[layer].reshape(1, 2 * d), conv_w[layer].reshape(CONV_K, CONV_WIDTH),
            w_conv_out[layer].astype(BF16), norm_q[layer].reshape(1, Q_LORA),
            norm_kv[layer].reshape(1, KV_LORA), wuq.astype(BF16), wuqs.astype(BF16),
            wuk.astype(BF16), wuv.astype(BF16), c_tab, s_tab)
        i = layer // 2
        last = layer == DEPTH - 1
        this_expert_w = expert_w
        next_is_moe = not last and (layer + 1) % 2 == 1
        nxt = (layer + 1) // 2
        o, expert_w = _attention(
            q, k, v, batch, seq,
            (moe_w_gate[nxt], moe_w_up[nxt], moe_w_down[nxt]) if next_is_moe else ())
        merge_args = (o, gaya, gb, h, w_attn_out[layer].astype(BF16),
                      w_mix_out[layer].astype(BF16), norm_ffn[layer].reshape(1, d))
        if layer % 2 == 0:
            h_mid, xn = _merge(*merge_args)
            h = _dense_ffn(xn, h_mid, dense_w_gate[i].astype(BF16), dense_w_up[i].astype(BF16),
                           dense_w_down[i].astype(BF16))
            if last:
                h = _final_norm(h, norm_final.reshape(1, d))
        else:
            wr = jnp.pad(router_w[i], ((0, 0), (0, LANES - N_EXPERTS)))
            wr_hi = wr.astype(BF16)
            wr_lo = (wr - wr_hi.astype(F32)).astype(BF16)
            h_mid, xp, route, route_t, counts = _merge(
                *merge_args, router=jnp.concatenate([wr_hi, wr_lo], axis=1))
            wg, wu, wd = this_expert_w
            h = _moe_ffn(xp, route, route_t, counts, h_mid, wg, wu, wd,
                         norm_final.reshape(1, d), last)
    return h.reshape(batch, seq, d)
```

```python
import functools

import jax
import jax.numpy as jnp
from jax import lax
from jax.experimental import pallas as pl
from jax.experimental.pallas import tpu as pltpu

F32 = jnp.float32
BF16 = jnp.bfloat16

D_MODEL = 1024
DEPTH = 2
CHUNK = 64
CONV_WIDTH = 1024
CONV_K = 3
MLA_HEADS = 16
QK_NOPE = 64
QK_ROPE = 32
V_DIM = 64
Q_LORA = 256
KV_LORA = 128
ROPE_THETA = 10000.0
N_EXPERTS = 8
TOP_K = 2
EPS = 1e-6
LOG2_E = 1.4426950408889634

LANES = 128
SUBLANES = 8
HEAD_PAD = LANES
HALF_ROPE = QK_ROPE // 2

C_GATE_B = 0
C_GATE_C = C_GATE_B + CONV_WIDTH
C_U = C_GATE_C + CONV_WIDTH
C_Q = C_U + CONV_WIDTH
C_KV = C_Q + Q_LORA
C_KROPE = C_KV + KV_LORA

TM_FRONT = 512
TM_MIX = 512
TM_FFN = 512
TM_ROUTE = 512
TM_GROUP = 512
DMA_UNROLL = 8
TQ = 512
HEADS_PER_STEP = 4
VMEM_LIMIT = 48 * 1024 * 1024


def _rms(x, g):
    inv = lax.rsqrt(jnp.mean(x * x, axis=-1, keepdims=True) + EPS)
    return (x * inv) * g


def _dot(a, b):
    return jnp.dot(a, b, preferred_element_type=F32)


def _const_spec(shape, single_buffer=False, layer=None):
    mode = pl.Buffered(1) if single_buffer else None
    if layer is None:
        nd = len(shape)
        return pl.BlockSpec(shape, lambda *_: (0,) * nd, pipeline_mode=mode)
    nd = len(shape) - 1
    return pl.BlockSpec((None,) + tuple(shape[1:]), lambda *_: (layer,) + (0,) * nd,
                        pipeline_mode=mode)


def _rope_kernel(n_cast, pos_ref, invf_ref, *rest):
    cos_ref, sin_ref, nsin_ref = rest[n_cast:n_cast + 3]
    for src, dst in zip(rest[:n_cast], rest[n_cast + 3:]):
        dst[...] = src[...].astype(dst.dtype)
    ang = pos_ref[...].astype(F32) * invf_ref[...]
    c = jnp.cos(ang)
    s = jnp.sin(ang)
    cos_ref[...] = c
    sin_ref[...] = s
    nsin_ref[...] = -s


def _rope_tables(positions, to_cast):
    t = positions.size
    rows = t * HALF_ROPE // LANES
    pos_rep = jnp.repeat(positions.reshape(t), HALF_ROPE).reshape(rows, LANES)
    inv_freq = ROPE_THETA ** (-jnp.arange(0, QK_ROPE, 2, dtype=F32) / QK_ROPE)
    invf = jnp.tile(inv_freq, LANES // HALF_ROPE).reshape(1, LANES)
    blk = 512
    steps = rows // blk
    spec = pl.BlockSpec((blk, LANES), lambda i: (i, 0))
    slabs = [w.reshape(-1, w.shape[-1]) for w in to_cast]
    slab_specs = [pl.BlockSpec((s.shape[0] // steps, s.shape[1]), lambda i: (i, 0)) for s in slabs]
    outs = pl.pallas_call(
        functools.partial(_rope_kernel, len(slabs)),
        out_shape=[jax.ShapeDtypeStruct((rows, LANES), F32)] * 3
        + [jax.ShapeDtypeStruct(s.shape, BF16) for s in slabs],
        grid=(steps,),
        in_specs=[spec, _const_spec((1, LANES))] + slab_specs,
        out_specs=[spec] * 3 + slab_specs,
        compiler_params=pltpu.CompilerParams(vmem_limit_bytes=VMEM_LIMIT),
        name="rope_tables",
    )(pos_rep, invf, *slabs)
    cos, sin, nsin = outs[:3]
    casted = [o.reshape(w.shape) for o, w in zip(outs[3:], to_cast)]
    cos = cos.reshape(t, HALF_ROPE)
    sin = sin.reshape(t, HALF_ROPE)
    nsin = nsin.reshape(t, HALF_ROPE)
    ones = jnp.ones((t, QK_NOPE), F32)
    zeros = jnp.zeros((t, QK_NOPE), F32)
    pad = jnp.zeros((t, HEAD_PAD - QK_NOPE - QK_ROPE), F32)
    c_tab = jnp.concatenate([ones, cos, cos, pad], axis=1)
    s_tab = jnp.concatenate([zeros, nsin, sin, pad], axis=1)
    return c_tab, s_tab, casted


def _front_kernel(tiles_per_seq, x_ref, gmix_ref, win_ref, wkr_ref, wgl_ref, bgate_ref, convw_ref, wco_ref,
                  gq_ref, gkv_ref, wuq_ref, wuqs_ref, wuk_ref, wuv_ref, c_ref, s_ref,
                  q_ref, k_ref, v_ref, gaya_ref, gb_ref, zbuf):
    tm = x_ref.shape[0]
    i = pl.program_id(0)
    xn = _rms(x_ref[...], gmix_ref[...]).astype(BF16)

    def proj(lo, width):
        return _dot(xn, win_ref[:, lo:lo + width])

    z = proj(C_GATE_C, CONV_WIDTH) * proj(C_U, CONV_WIDTH)

    @pl.when(i % tiles_per_seq == 0)
    def _():
        zbuf[0:SUBLANES, :] = jnp.zeros((SUBLANES, CONV_WIDTH), F32)

    zbuf[SUBLANES:SUBLANES + tm, :] = z
    cw = convw_ref[...]
    conv = (cw[2:3, :] * z
            + cw[1:2, :] * zbuf[SUBLANES - 1:SUBLANES - 1 + tm, :]
            + cw[0:1, :] * zbuf[SUBLANES - 2:SUBLANES - 2 + tm, :])
    zbuf[0:SUBLANES, :] = zbuf[tm:tm + SUBLANES, :]
    y_a = _dot((proj(C_GATE_B, CONV_WIDTH) * conv).astype(BF16), wco_ref[...])

    gate = jax.nn.sigmoid(_dot(xn, wgl_ref[...]) + bgate_ref[...])
    gaya_ref[...] = (gate[:, :D_MODEL] * y_a).astype(BF16)
    gb_ref[...] = gate[:, D_MODEL:].astype(BF16)

    c_tab = c_ref[...]
    s_tab = s_ref[...]
    c_all = jnp.tile(c_tab, (1, MLA_HEADS))
    s_all = jnp.tile(s_tab, (1, MLA_HEADS))
    cqn = _rms(proj(C_Q, Q_LORA), gq_ref[...]).astype(BF16)
    q = _dot(cqn, wuq_ref[...]) * c_all + _dot(cqn, wuqs_ref[...]) * s_all
    q_ref[...] = (q * (LOG2_E * (QK_NOPE + QK_ROPE) ** -0.5)).astype(BF16)

    ckvn = _rms(proj(C_KV, KV_LORA), gkv_ref[...]).astype(BF16)
    k_rot = _dot(xn, wkr_ref[:, :LANES]) * c_tab + _dot(xn, wkr_ref[:, LANES:]) * s_tab
    k_ref[...] = (_dot(ckvn, wuk_ref[...]) + jnp.tile(k_rot, (1, MLA_HEADS))).astype(BF16)
    v_ref[...] = _dot(ckvn, wuv_ref[...]).astype(BF16)


def _front(h, seq, layer, gmix, win, wkr, wgl, bgate, convw, wco_all, gq, gkv, wuq, wuqs, wuk, wuv,
           c_tab, s_tab):
    t = h.shape[0]
    tm = TM_FRONT
    hp = MLA_HEADS * HEAD_PAD
    row = lambda w: pl.BlockSpec((tm, w), lambda i: (i, 0))
    return pl.pallas_call(
        functools.partial(_front_kernel, seq // tm),
        out_shape=[jax.ShapeDtypeStruct((t, hp), BF16),
                   jax.ShapeDtypeStruct((t, hp), BF16),
                   jax.ShapeDtypeStruct((t, MLA_HEADS * V_DIM), BF16),
                   jax.ShapeDtypeStruct((t, D_MODEL), BF16),
                   jax.ShapeDtypeStruct((t, D_MODEL), BF16)],
        grid=(t // tm,),
        in_specs=[row(D_MODEL)]
        + [_const_spec(a.shape, single_buffer=True) for a in (gmix, win, wkr, wgl, bgate, convw)]
        + [_const_spec(wco_all.shape, single_buffer=True, layer=layer)]
        + [_const_spec(a.shape, single_buffer=True) for a in (gq, gkv, wuq, wuqs, wuk, wuv)]
        + [row(LANES), row(LANES)],
        out_specs=[row(hp), row(hp), row(MLA_HEADS * V_DIM), row(D_MODEL), row(D_MODEL)],
        scratch_shapes=[pltpu.VMEM((tm + SUBLANES, CONV_WIDTH), F32)],
        compiler_params=pltpu.CompilerParams(
            dimension_semantics=("arbitrary",), vmem_limit_bytes=VMEM_LIMIT),
        name="mixer_front",
    )(h, gmix, win, wkr, wgl, bgate, convw, wco_all, gq, gkv, wuq, wuqs, wuk, wuv, c_tab, s_tab)


def _attn_kernel(tq, n_cast, q_ref, k_ref, v_ref, *rest):
    o_ref = rest[n_cast]
    for src, dst in zip(rest[:n_cast], rest[n_cast + 1:]):
        dst[...] = src[...].astype(dst.dtype)
    seq = q_ref.shape[0]
    n_heads = q_ref.shape[1] // HEAD_PAD
    row_chunk = lax.broadcasted_iota(jnp.int32, (tq, tq), 0) // CHUNK
    col_chunk = lax.broadcasted_iota(jnp.int32, (tq, tq), 1) // CHUNK
    diag_mask = col_chunk <= row_chunk
    v_lane = lax.broadcasted_iota(jnp.int32, (seq, 2 * V_DIM), 1)
    contract_last = (((1,), (1,)), ((), ()))

    for hh in range(n_heads):
        cols = slice(hh * HEAD_PAD, (hh + 1) * HEAD_PAD)
        o_cols = slice(hh * V_DIM, (hh + 1) * V_DIM)
        odd = hh % 2
        v_pair = v_ref[:, (hh - odd) * V_DIM:(hh - odd + 2) * V_DIM]
        own = (v_lane >= V_DIM) if odd else (v_lane < V_DIM)
        v_ext = jnp.where(own, v_pair, jnp.ones((), v_pair.dtype))
        for qi in reversed(range(seq // tq)):
            q_lo = qi * tq
            q = q_ref[q_lo:q_lo + tq, cols]
            s_diag = lax.dot_general(q, k_ref[q_lo:q_lo + tq, cols], contract_last,
                                     preferred_element_type=F32)
            s_diag = jnp.where(diag_mask, s_diag, -jnp.inf)
            m = jnp.max(s_diag, axis=-1, keepdims=True)
            if qi:
                s_past = lax.dot_general(q, k_ref[0:q_lo, cols], contract_last,
                                         preferred_element_type=F32)
                m = jnp.maximum(m, jnp.max(s_past, axis=-1, keepdims=True))
            acc = _dot(jnp.exp2(s_diag - m).astype(BF16), v_ext[q_lo:q_lo + tq, :])
            if qi:
                acc = acc + _dot(jnp.exp2(s_past - m).astype(BF16), v_ext[0:q_lo, :])
            out = (acc / pltpu.roll(acc, V_DIM, axis=1)).astype(o_ref.dtype)
            o_ref[q_lo:q_lo + tq, o_cols] = out[:, odd * V_DIM:(odd + 1) * V_DIM]


def _attention(q, k, v, batch, seq, to_cast=()):
    t = q.shape[0]
    hps = HEADS_PER_STEP
    groups = MLA_HEADS // hps
    steps = batch * groups
    slabs = [w.reshape(-1, w.shape[-1]) for w in to_cast]
    slab_specs = [pl.BlockSpec((s.shape[0] // steps, s.shape[1]), lambda b, p: (b * groups + p, 0))
                  for s in slabs]
    outs = pl.pallas_call(
        functools.partial(_attn_kernel, TQ, len(slabs)),
        out_shape=[jax.ShapeDtypeStruct((t, MLA_HEADS * V_DIM), BF16)]
        + [jax.ShapeDtypeStruct(s.shape, BF16) for s in slabs],
        grid=(batch, groups),
        in_specs=[pl.BlockSpec((seq, hps * HEAD_PAD), lambda b, p: (b, p)),
                  pl.BlockSpec((seq, hps * HEAD_PAD), lambda b, p: (b, p)),
                  pl.BlockSpec((seq, hps * V_DIM), lambda b, p: (b, p))] + slab_specs,
        out_specs=[pl.BlockSpec((seq, hps * V_DIM), lambda b, p: (b, p))] + slab_specs,
        compiler_params=pltpu.CompilerParams(
            dimension_semantics=("parallel", "parallel"), vmem_limit_bytes=VMEM_LIMIT),
        name="chunk_attention",
    )(q, k, v, *slabs)
    return outs[0], [o.reshape(w.shape) for o, w in zip(outs[1:], to_cast)]


def _merge_body(o_ref, gaya_ref, gb_ref, h_ref, wo_ref, wmix_ref, gffn_ref, hmid_ref):
    y_b = _dot(o_ref[...], wo_ref[...])
    merged = gaya_ref[...].astype(F32) + gb_ref[...].astype(F32) * y_b
    h_mid = h_ref[...] + _dot(merged.astype(BF16), wmix_ref[...])
    hmid_ref[...] = h_mid
    return _rms(h_mid, gffn_ref[...])


def _merge_kernel(o_ref, gaya_ref, gb_ref, h_ref, wo_ref, wmix_ref, gffn_ref, hmid_ref, xn_ref):
    xn = _merge_body(o_ref, gaya_ref, gb_ref, h_ref, wo_ref, wmix_ref, gffn_ref, hmid_ref)
    xn_ref[...] = xn.astype(BF16)


R_W1, R_W2, R_E1, R_E2, R_RANK1, R_RANK2 = range(6)


def _merge_router_kernel(o_ref, gaya_ref, gb_ref, h_ref, wo_ref, wmix_ref, gffn_ref,
                         wr_ref, hmid_ref, xp_ref, route_ref, route_t_ref, counts_ref, zeros_ref):
    xn = _merge_body(o_ref, gaya_ref, gb_ref, h_ref, wo_ref, wmix_ref, gffn_ref, hmid_ref)
    tm = xn.shape[0]
    zeros_ref[...] = jnp.zeros(zeros_ref.shape, zeros_ref.dtype)

    @pl.when(pl.program_id(0) == 0)
    def _():
        counts_ref[...] = jnp.zeros(counts_ref.shape, F32)

    x_hi = xn.astype(BF16)
    xp_ref[...] = x_hi.astype(F32)

    x_lo = (xn - x_hi.astype(F32)).astype(BF16)
    prod = _dot(jnp.concatenate([x_hi, x_lo], axis=0), wr_ref[...])
    logits = prod[:tm, :LANES] + (prod[:tm, LANES:] + prod[tm:, :LANES])
    lane = lax.broadcasted_iota(jnp.int32, logits.shape, 1)
    lg = jnp.where(lane < N_EXPERTS, logits, -jnp.inf)
    m1 = jnp.max(lg, axis=-1, keepdims=True)
    i1 = jnp.min(jnp.where(lg == m1, lane, LANES), axis=-1, keepdims=True)
    lg2 = jnp.where(lane == i1, -jnp.inf, lg)
    m2 = jnp.max(lg2, axis=-1, keepdims=True)
    i2 = jnp.min(jnp.where(lg2 == m2, lane, LANES), axis=-1, keepdims=True)
    e2 = jnp.exp(m2 - m1)
    den = 1.0 + e2

    onehot = jnp.where(lane == i1, 1.0, jnp.where(lane == i2, 1.0, 0.0))
    rows = lax.broadcasted_iota(jnp.int32, (tm, tm), 0)
    cols = lax.broadcasted_iota(jnp.int32, (tm, tm), 1)
    tri = jnp.where(rows > cols, 1.0, 0.0).astype(BF16)
    pos = _dot(tri, onehot.astype(BF16)) + counts_ref[...]
    rank1 = jnp.sum(jnp.where(lane == i1, pos, 0.0), axis=-1, keepdims=True)
    rank2 = jnp.sum(jnp.where(lane == i2, pos, 0.0), axis=-1, keepdims=True)
    counts_ref[...] += jnp.sum(onehot, axis=0, keepdims=True)

    record = jnp.zeros(logits.shape, F32)
    for lane_id, val in ((R_W1, 1.0 / den), (R_W2, e2 / den), (R_E1, i1.astype(F32)),
                         (R_E2, i2.astype(F32)), (R_RANK1, rank1), (R_RANK2, rank2)):
        record = jnp.where(lane == lane_id, val, record)
    route_ref[...] = record
    route_t_ref[...] = jnp.transpose(record)[:SUBLANES, :]


def _sorted_rows(t):
    return (TOP_K * t // TM_GROUP + N_EXPERTS) * TM_GROUP


def _merge(o, gaya, gb, h, layer, wo, wmix, gffn, router=None):
    t = h.shape[0]
    tm = TM_MIX
    row = lambda w: pl.BlockSpec((tm, w), lambda i: (i, 0))
    in_specs = [row(D_MODEL)] * 4 + [_const_spec(wo.shape, layer=layer),
                                     _const_spec(wmix.shape, layer=layer),
                                     _const_spec(gffn.shape)]
    args = [o, gaya, gb, h, wo, wmix, gffn]
    if router is None:
        body = _merge_kernel
        out_shape = [jax.ShapeDtypeStruct((t, D_MODEL), F32),
                     jax.ShapeDtypeStruct((t, D_MODEL), BF16)]
        out_specs = [row(D_MODEL), row(D_MODEL)]
    else:
        body = _merge_router_kernel
        in_specs += [_const_spec(router.shape)]
        args += [router]
        out_shape = [jax.ShapeDtypeStruct((t, D_MODEL), F32),
                     jax.ShapeDtypeStruct((t, D_MODEL), F32),
                     jax.ShapeDtypeStruct((t, LANES), F32),
                     jax.ShapeDtypeStruct((SUBLANES, t), F32),
                     jax.ShapeDtypeStruct((1, LANES), F32),
                     jax.ShapeDtypeStruct((_sorted_rows(t), D_MODEL), F32)]
        out_specs = [row(D_MODEL), row(D_MODEL), row(LANES),
                     pl.BlockSpec((SUBLANES, tm), lambda i: (0, i)), _const_spec((1, LANES)),
                     pl.BlockSpec((_sorted_rows(t) // (t // tm), D_MODEL), lambda i: (i, 0))]
    return pl.pallas_call(
        body, out_shape=out_shape, grid=(t // tm,), in_specs=in_specs, out_specs=out_specs,
        compiler_params=pltpu.CompilerParams(
            dimension_semantics=("arbitrary",), vmem_limit_bytes=VMEM_LIMIT),
        name="merge_router" if router is not None else "merge",
    )(*args)


def _swiglu(x, wg, wu, wd):
    g = _dot(x, wg)
    u = _dot(x, wu)
    return _dot((g * jax.nn.sigmoid(g) * u).astype(BF16), wd)


def _dense_ffn_kernel(x_ref, hmid_ref, wg_ref, wu_ref, wd_ref, out_ref):
    out_ref[...] = hmid_ref[...] + _swiglu(x_ref[...], wg_ref[...], wu_ref[...], wd_ref[...])


def _dense_ffn(xn, h_mid, wg, wu, wd):
    t = xn.shape[0]
    tm = TM_FFN
    row = pl.BlockSpec((tm, D_MODEL), lambda i: (i, 0))
    return pl.pallas_call(
        _dense_ffn_kernel,
        out_shape=jax.ShapeDtypeStruct((t, D_MODEL), F32),
        grid=(t // tm,),
        in_specs=[row, row] + [_const_spec(w.shape, single_buffer=True) for w in (wg, wu, wd)],
        out_specs=row,
        compiler_params=pltpu.CompilerParams(
            dimension_semantics=("parallel",), vmem_limit_bytes=VMEM_LIMIT),
        name="dense_ffn",
    )(xn, h_mid, wg, wu, wd)


def _row_copy(src_ref, src_row, dst_ref, dst_row, sem):
    return pltpu.make_async_copy(src_ref.at[pl.ds(src_row, 1)], dst_ref.at[pl.ds(dst_row, 1)], sem)


def _for_each_row(tm, fn):
    def body(r, carry):
        fn(r)
        return carry
    lax.fori_loop(0, tm, body, 0, unroll=DMA_UNROLL)


def _drain_row_copies(src_ref, dst_ref, sem, n):
    def body(_, carry):
        _row_copy(src_ref, 0, dst_ref, 0, sem).wait()
        return carry
    lax.fori_loop(0, n, body, 0, unroll=DMA_UNROLL)


def _dispatch_kernel(slots_ref, x_ref, _, xs_ref, sem):
    tm = x_ref.shape[0]

    def issue(r):
        for k in range(TOP_K):
            _row_copy(x_ref, r, xs_ref, slots_ref[0, 0, TOP_K * r + k], sem).start()

    _for_each_row(tm, issue)
    _drain_row_copies(x_ref, xs_ref, sem, TOP_K * tm)


def _dispatch(xp, slots, xs0):
    t, w = xp.shape
    tm = TM_ROUTE
    return pl.pallas_call(
        _dispatch_kernel,
        out_shape=jax.ShapeDtypeStruct(xs0.shape, xp.dtype),
        grid=(t // tm,),
        in_specs=[pl.BlockSpec((1, 1, TOP_K * tm), lambda i: (i, 0, 0), memory_space=pltpu.SMEM),
                  pl.BlockSpec((tm, w), lambda i: (i, 0)),
                  pl.BlockSpec(memory_space=pl.ANY)],
        out_specs=pl.BlockSpec(memory_space=pl.ANY),
        scratch_shapes=[pltpu.SemaphoreType.DMA],
        input_output_aliases={2: 0},
        compiler_params=pltpu.CompilerParams(dimension_semantics=("arbitrary",)),
        name="moe_dispatch",
    )(slots, xp, xs0)


def _group_ffn_kernel(te_ref, tb_ref, nv_ref, x_ref, wg_ref, wu_ref, wd_ref, y_ref):
    del te_ref, tb_ref
    live = pl.program_id(0) < nv_ref[0]

    @pl.when(live)
    def _():
        y_ref[...] = _swiglu(x_ref[...].astype(BF16), wg_ref[...], wu_ref[...], wd_ref[...])

    @pl.when(jnp.logical_not(live))
    def _():
        y_ref[...] = jnp.zeros(y_ref.shape, y_ref.dtype)


def _group_ffn(xs, tile_expert, tile_block, n_valid, wg, wu, wd):
    n_rows, w = xs.shape
    tm = TM_GROUP
    _, _, d_ff = wg.shape
    return pl.pallas_call(
        _group_ffn_kernel,
        out_shape=jax.ShapeDtypeStruct((n_rows, D_MODEL), F32),
        grid_spec=pltpu.PrefetchScalarGridSpec(
            num_scalar_prefetch=3,
            grid=(n_rows // tm,),
            in_specs=[pl.BlockSpec((tm, w), lambda i, te, tb, nv: (tb[i], 0)),
                      pl.BlockSpec((None, D_MODEL, d_ff), lambda i, te, tb, nv: (te[i], 0, 0)),
                      pl.BlockSpec((None, D_MODEL, d_ff), lambda i, te, tb, nv: (te[i], 0, 0)),
                      pl.BlockSpec((None, d_ff, D_MODEL), lambda i, te, tb, nv: (te[i], 0, 0))],
            out_specs=pl.BlockSpec((tm, D_MODEL), lambda i, te, tb, nv: (i, 0))),
        compiler_params=pltpu.CompilerParams(
            dimension_semantics=("arbitrary",), vmem_limit_bytes=VMEM_LIMIT),
        name="moe_group_ffn",
    )(tile_expert, tile_block, n_valid, xs, wg, wu, wd)


def _combine_kernel(final_norm, slots_ref, route_ref, hmid_ref, gfin_ref, ys_ref, out_ref,
                    ybuf, sem):
    tm = hmid_ref.shape[0]

    def issue(r):
        for k in range(TOP_K):
            _row_copy(ys_ref, slots_ref[0, 0, TOP_K * r + k], ybuf.at[k], r, sem).start()

    _for_each_row(tm, issue)
    _drain_row_copies(ys_ref, ybuf.at[0], sem, TOP_K * tm)
    route = route_ref[...]
    h = (hmid_ref[...] + route[:, R_W1:R_W1 + 1] * ybuf[0]) + route[:, R_W2:R_W2 + 1] * ybuf[1]
    out_ref[...] = _rms(h, gfin_ref[...]) if final_norm else h


def _combine(ys, slots, route, h_mid, gfin, final_norm):
    t = h_mid.shape[0]
    tm = TM_ROUTE
    row = lambda w: pl.BlockSpec((tm, w), lambda i: (i, 0))
    return pl.pallas_call(
        functools.partial(_combine_kernel, final_norm),
        out_shape=jax.ShapeDtypeStruct((t, D_MODEL), F32),
        grid=(t // tm,),
        in_specs=[pl.BlockSpec((1, 1, TOP_K * tm), lambda i: (i, 0, 0), memory_space=pltpu.SMEM),
                  row(LANES), row(D_MODEL), _const_spec(gfin.shape),
                  pl.BlockSpec(memory_space=pl.ANY)],
        out_specs=row(D_MODEL),
        scratch_shapes=[pltpu.VMEM((TOP_K, tm, D_MODEL), F32), pltpu.SemaphoreType.DMA],
        compiler_params=pltpu.CompilerParams(dimension_semantics=("arbitrary",)),
        name="moe_combine",
    )(slots, route, h_mid, gfin, ys)


def _moe_ffn(xp, route, route_t, counts, xs0, h_mid, wg, wu, wd, gfin, final_norm):
    t = h_mid.shape[0]
    tm = TM_GROUP
    n_tiles = xs0.shape[0] // tm
    tiles_per_expert = (counts[0, :N_EXPERTS].astype(jnp.int32) + tm - 1) // tm
    tile_end = jnp.cumsum(tiles_per_expert)
    row_start = (tile_end - tiles_per_expert) * tm
    experts = jnp.arange(N_EXPERTS, dtype=jnp.int32)

    def slot(expert_row, rank_row):
        expert = route_t[expert_row].astype(jnp.int32)
        start = jnp.sum(jnp.where(expert[:, None] == experts[None, :], row_start[None, :], 0), axis=1)
        return start + route_t[rank_row].astype(jnp.int32)

    slots = jnp.stack([slot(R_E1, R_RANK1), slot(R_E2, R_RANK2)], axis=1)
    slots = slots.reshape(t // TM_ROUTE, 1, TOP_K * TM_ROUTE)
    n_valid = tile_end[-1:]
    tile_block = jnp.minimum(jnp.arange(n_tiles, dtype=jnp.int32), n_valid - 1)
    tile_expert = jnp.sum((tile_block[:, None] >= tile_end[None, :]).astype(jnp.int32), axis=1)
    tile_expert = jnp.minimum(tile_expert, N_EXPERTS - 1)
    xs = _dispatch(xp, slots, xs0)
    ys = _group_ffn(xs, tile_expert, tile_block, n_valid, wg, wu, wd)
    return _combine(ys, slots, route, h_mid, gfin, final_norm)


def _final_norm_kernel(h_ref, g_ref, out_ref):
    out_ref[...] = _rms(h_ref[...], g_ref[...])


def _final_norm(h, g):
    t = h.shape[0]
    row = pl.BlockSpec((TM_FFN, D_MODEL), lambda i: (i, 0))
    return pl.pallas_call(
        _final_norm_kernel, out_shape=jax.ShapeDtypeStruct((t, D_MODEL), F32),
        grid=(t // TM_FFN,), in_specs=[row, _const_spec(g.shape)], out_specs=row,
        name="final_norm",
    )(h, g)


def _pack_w_in(w):
    d = w.shape[0]
    kr_a = w[:, C_KROPE:C_KROPE + HALF_ROPE]
    kr_b = w[:, C_KROPE + HALF_ROPE:C_KROPE + QK_ROPE]
    z_lo = jnp.zeros((d, QK_NOPE), w.dtype)
    z_hi = jnp.zeros((d, HEAD_PAD - QK_NOPE - QK_ROPE), w.dtype)
    w_kr = jnp.concatenate([z_lo, kr_a, kr_b, z_hi, z_lo, kr_b, kr_a, z_hi], axis=1)
    return (w[:, :C_KROPE].astype(BF16), w_kr.astype(BF16), w[:, C_KROPE + QK_ROPE:].astype(BF16))


def _pack_w_uq(w):
    w3 = w.reshape(Q_LORA, MLA_HEADS, QK_NOPE + QK_ROPE)
    nope = w3[..., :QK_NOPE]
    r_a = w3[..., QK_NOPE:QK_NOPE + HALF_ROPE]
    r_b = w3[..., QK_NOPE + HALF_ROPE:]
    pad = jnp.zeros((Q_LORA, MLA_HEADS, HEAD_PAD - QK_NOPE - QK_ROPE), w.dtype)
    plain = jnp.concatenate([nope, r_a, r_b, pad], axis=-1)
    swapped = jnp.concatenate([jnp.zeros_like(nope), r_b, r_a, pad], axis=-1)
    return (plain.reshape(Q_LORA, MLA_HEADS * HEAD_PAD),
            swapped.reshape(Q_LORA, MLA_HEADS * HEAD_PAD))


def _pack_w_ukv(w):
    w3 = w.reshape(KV_LORA, MLA_HEADS, QK_NOPE + V_DIM)
    k = jnp.concatenate([w3[..., :QK_NOPE],
                         jnp.zeros((KV_LORA, MLA_HEADS, HEAD_PAD - QK_NOPE), w.dtype)], axis=-1)
    return (k.reshape(KV_LORA, MLA_HEADS * HEAD_PAD),
            w3[..., QK_NOPE:].reshape(KV_LORA, MLA_HEADS * V_DIM))


def kernel(x, positions, norm_mix, w_in, b_gate, conv_w, w_conv_out, norm_q, norm_kv, w_uq, w_ukv,
           w_attn_out, w_mix_out, norm_ffn, dense_w_gate, dense_w_up, dense_w_down, router_w,
           moe_w_gate, moe_w_up, moe_w_down, norm_final):
    batch, seq, d = x.shape
    t = batch * seq
    c_tab, s_tab, (wco_all, wo_all, wmix_all) = _rope_tables(
        positions, (w_conv_out, w_attn_out, w_mix_out))
    h = x.reshape(t, d)
    expert_w = None
    for layer in range(DEPTH):
        wuq, wuqs = _pack_w_uq(w_uq[layer])
        wuk, wuv = _pack_w_ukv(w_ukv[layer])
        q, k, v, gaya, gb = _front(
            h, seq, layer, norm_mix[layer].reshape(1, d), *_pack_w_in(w_in[layer]),
            b_gate[layer].reshape(1, 2 * d), conv_w[layer].reshape(CONV_K, CONV_WIDTH),
            wco_all, norm_q[layer].reshape(1, Q_LORA),
            norm_kv[layer].reshape(1, KV_LORA), wuq.astype(BF16), wuqs.astype(BF16),
            wuk.astype(BF16), wuv.astype(BF16), c_tab, s_tab)
        i = layer // 2
        last = layer == DEPTH - 1
        this_expert_w = expert_w
        next_is_moe = not last and (layer + 1) % 2 == 1
        nxt = (layer + 1) // 2
        o, expert_w = _attention(
            q, k, v, batch, seq,
            (moe_w_gate[nxt], moe_w_up[nxt], moe_w_down[nxt]) if next_is_moe else ())
        merge_args = (o, gaya, gb, h, layer, wo_all, wmix_all, norm_ffn[layer].reshape(1, d))
        if layer % 2 == 0:
            h_mid, xn = _merge(*merge_args)
            h = _dense_ffn(xn, h_mid, dense_w_gate[i].astype(BF16), dense_w_up[i].astype(BF16),
                           dense_w_down[i].astype(BF16))
            if last:
                h = _final_norm(h, norm_final.reshape(1, d))
        else:
            wr = jnp.pad(router_w[i], ((0, 0), (0, LANES - N_EXPERTS)))
            wr_hi = wr.astype(BF16)
            wr_lo = (wr - wr_hi.astype(F32)).astype(BF16)
            h_mid, xp, route, route_t, counts, xs0 = _merge(
                *merge_args, router=jnp.concatenate([wr_hi, wr_lo], axis=1))
            wg, wu, wd = this_expert_w
            h = _moe_ffn(xp, route, route_t, counts, xs0, h_mid, wg, wu, wd,
                         norm_final.reshape(1, d), last)
    return h.reshape(batch, seq, d)
```

```python
import functools

import jax
import jax.numpy as jnp
from jax import lax
from jax.experimental import pallas as pl
from jax.experimental.pallas import tpu as pltpu

F32 = jnp.float32
BF16 = jnp.bfloat16

D_MODEL = 1024
DEPTH = 2
CHUNK = 64
CONV_WIDTH = 1024
CONV_K = 3
MLA_HEADS = 16
QK_NOPE = 64
QK_ROPE = 32
V_DIM = 64
Q_LORA = 256
KV_LORA = 128
ROPE_THETA = 10000.0
N_EXPERTS = 8
TOP_K = 2
EPS = 1e-6
LOG2_E = 1.4426950408889634

LANES = 128
SUBLANES = 8
HEAD_PAD = LANES
HALF_ROPE = QK_ROPE // 2

C_GATE_B = 0
C_GATE_C = C_GATE_B + CONV_WIDTH
C_U = C_GATE_C + CONV_WIDTH
C_Q = C_U + CONV_WIDTH
C_KV = C_Q + Q_LORA
C_KROPE = C_KV + KV_LORA

TM_FRONT = 512
TM_MIX = 512
TM_FFN = 512
TM_ROUTE = 512
TM_GROUP = 512
DMA_UNROLL = 8
TQ = 512
HEADS_PER_STEP = 4
VMEM_LIMIT = 48 * 1024 * 1024


def _rms(x, g):
    inv = lax.rsqrt(jnp.mean(x * x, axis=-1, keepdims=True) + EPS)
    return (x * inv) * g


def _dot(a, b):
    return jnp.dot(a, b, preferred_element_type=F32)


def _const_spec(shape, single_buffer=False, layer=None):
    mode = pl.Buffered(1) if single_buffer else None
    if layer is None:
        nd = len(shape)
        return pl.BlockSpec(shape, lambda *_: (0,) * nd, pipeline_mode=mode)
    nd = len(shape) - 1
    return pl.BlockSpec((None,) + tuple(shape[1:]), lambda *_: (layer,) + (0,) * nd,
                        pipeline_mode=mode)


def _rope_kernel(n_cast, pos_ref, invf_ref, *rest):
    cos_ref, sin_ref, nsin_ref = rest[n_cast:n_cast + 3]
    for src, dst in zip(rest[:n_cast], rest[n_cast + 3:]):
        dst[...] = src[...].astype(dst.dtype)
    ang = pos_ref[...].astype(F32) * invf_ref[...]
    c = jnp.cos(ang)
    s = jnp.sin(ang)
    cos_ref[...] = c
    sin_ref[...] = s
    nsin_ref[...] = -s


def _rope_tables(positions, to_cast):
    t = positions.size
    rows = t * HALF_ROPE // LANES
    pos_rep = jnp.repeat(positions.reshape(t), HALF_ROPE).reshape(rows, LANES)
    inv_freq = ROPE_THETA ** (-jnp.arange(0, QK_ROPE, 2, dtype=F32) / QK_ROPE)
    invf = jnp.tile(inv_freq, LANES // HALF_ROPE).reshape(1, LANES)
    blk = 512
    steps = rows // blk
    spec = pl.BlockSpec((blk, LANES), lambda i: (i, 0))
    slabs = [w.reshape(-1, w.shape[-1]) for w in to_cast]
    slab_specs = [pl.BlockSpec((s.shape[0] // steps, s.shape[1]), lambda i: (i, 0)) for s in slabs]
    outs = pl.pallas_call(
        functools.partial(_rope_kernel, len(slabs)),
        out_shape=[jax.ShapeDtypeStruct((rows, LANES), F32)] * 3
        + [jax.ShapeDtypeStruct(s.shape, BF16) for s in slabs],
        grid=(steps,),
        in_specs=[spec, _const_spec((1, LANES))] + slab_specs,
        out_specs=[spec] * 3 + slab_specs,
        compiler_params=pltpu.CompilerParams(vmem_limit_bytes=VMEM_LIMIT),
        name="rope_tables",
    )(pos_rep, invf, *slabs)
    cos, sin, nsin = outs[:3]
    casted = [o.reshape(w.shape) for o, w in zip(outs[3:], to_cast)]
    cos = cos.reshape(t, HALF_ROPE)
    sin = sin.reshape(t, HALF_ROPE)
    nsin = nsin.reshape(t, HALF_ROPE)
    ones = jnp.ones((t, QK_NOPE), F32)
    zeros = jnp.zeros((t, QK_NOPE), F32)
    pad = jnp.zeros((t, HEAD_PAD - QK_NOPE - QK_ROPE), F32)
    c_tab = jnp.concatenate([ones, cos, cos, pad], axis=1)
    s_tab = jnp.concatenate([zeros, nsin, sin, pad], axis=1)
    return c_tab, s_tab, casted


def _front_kernel(tiles_per_seq, x_ref, gmix_ref, win_ref, wkr_ref, wgl_ref, bgate_ref, convw_ref, wco_ref,
                  gq_ref, gkv_ref, wuq_ref, wuqs_ref, wuk_ref, wuv_ref, c_ref, s_ref,
                  q_ref, k_ref, v_ref, gaya_ref, gb_ref, zbuf):
    tm = x_ref.shape[0]
    i = pl.program_id(0)
    xn = _rms(x_ref[...], gmix_ref[...]).astype(BF16)

    def proj(lo, width):
        return _dot(xn, win_ref[:, lo:lo + width])

    z = proj(C_GATE_C, CONV_WIDTH) * proj(C_U, CONV_WIDTH)

    @pl.when(i % tiles_per_seq == 0)
    def _():
        zbuf[0:SUBLANES, :] = jnp.zeros((SUBLANES, CONV_WIDTH), F32)

    zbuf[SUBLANES:SUBLANES + tm, :] = z
    cw = convw_ref[...]
    conv = (cw[2:3, :] * z
            + cw[1:2, :] * zbuf[SUBLANES - 1:SUBLANES - 1 + tm, :]
            + cw[0:1, :] * zbuf[SUBLANES - 2:SUBLANES - 2 + tm, :])
    zbuf[0:SUBLANES, :] = zbuf[tm:tm + SUBLANES, :]
    y_a = _dot((proj(C_GATE_B, CONV_WIDTH) * conv).astype(BF16), wco_ref[...])

    gate = jax.nn.sigmoid(_dot(xn, wgl_ref[...]) + bgate_ref[...])
    gaya_ref[...] = (gate[:, :D_MODEL] * y_a).astype(BF16)
    gb_ref[...] = gate[:, D_MODEL:].astype(BF16)

    c_tab = c_ref[...]
    s_tab = s_ref[...]
    c_all = jnp.tile(c_tab, (1, MLA_HEADS))
    s_all = jnp.tile(s_tab, (1, MLA_HEADS))
    cqn = _rms(proj(C_Q, Q_LORA), gq_ref[...]).astype(BF16)
    q = _dot(cqn, wuq_ref[...]) * c_all + _dot(cqn, wuqs_ref[...]) * s_all
    q_ref[...] = (q * (LOG2_E * (QK_NOPE + QK_ROPE) ** -0.5)).astype(BF16)

    ckvn = _rms(proj(C_KV, KV_LORA), gkv_ref[...]).astype(BF16)
    k_rot = _dot(xn, wkr_ref[:, :LANES]) * c_tab + _dot(xn, wkr_ref[:, LANES:]) * s_tab
    k_ref[...] = (_dot(ckvn, wuk_ref[...]) + jnp.tile(k_rot, (1, MLA_HEADS))).astype(BF16)
    v_ref[...] = _dot(ckvn, wuv_ref[...]).astype(BF16)


def _front(h, seq, layer, gmix, win, wkr, wgl, bgate, convw, wco_all, gq, gkv, wuq, wuqs, wuk, wuv,
           c_tab, s_tab):
    t = h.shape[0]
    tm = TM_FRONT
    hp = MLA_HEADS * HEAD_PAD
    row = lambda w: pl.BlockSpec((tm, w), lambda i: (i, 0))
    return pl.pallas_call(
        functools.partial(_front_kernel, seq // tm),
        out_shape=[jax.ShapeDtypeStruct((t, hp), BF16),
                   jax.ShapeDtypeStruct((t, hp), BF16),
                   jax.ShapeDtypeStruct((t, MLA_HEADS * V_DIM), BF16),
                   jax.ShapeDtypeStruct((t, D_MODEL), BF16),
                   jax.ShapeDtypeStruct((t, D_MODEL), BF16)],
        grid=(t // tm,),
        in_specs=[row(D_MODEL)]
        + [_const_spec(a.shape, single_buffer=True) for a in (gmix, win, wkr, wgl, bgate, convw)]
        + [_const_spec(wco_all.shape, single_buffer=True, layer=layer)]
        + [_const_spec(a.shape, single_buffer=True) for a in (gq, gkv, wuq, wuqs, wuk, wuv)]
        + [row(LANES), row(LANES)],
        out_specs=[row(hp), row(hp), row(MLA_HEADS * V_DIM), row(D_MODEL), row(D_MODEL)],
        scratch_shapes=[pltpu.VMEM((tm + SUBLANES, CONV_WIDTH), F32)],
        compiler_params=pltpu.CompilerParams(
            dimension_semantics=("arbitrary",), vmem_limit_bytes=VMEM_LIMIT),
        name="mixer_front",
    )(h, gmix, win, wkr, wgl, bgate, convw, wco_all, gq, gkv, wuq, wuqs, wuk, wuv, c_tab, s_tab)


def _attn_kernel(tq, n_cast, q_ref, k_ref, v_ref, *rest):
    o_ref = rest[n_cast]
    for src, dst in zip(rest[:n_cast], rest[n_cast + 1:]):
        dst[...] = src[...].astype(dst.dtype)
    seq = q_ref.shape[0]
    n_heads = q_ref.shape[1] // HEAD_PAD
    row_chunk = lax.broadcasted_iota(jnp.int32, (tq, tq), 0) // CHUNK
    col_chunk = lax.broadcasted_iota(jnp.int32, (tq, tq), 1) // CHUNK
    diag_mask = col_chunk <= row_chunk
    v_lane = lax.broadcasted_iota(jnp.int32, (seq, 2 * V_DIM), 1)
    contract_last = (((1,), (1,)), ((), ()))

    for hh in range(n_heads):
        cols = slice(hh * HEAD_PAD, (hh + 1) * HEAD_PAD)
        o_cols = slice(hh * V_DIM, (hh + 1) * V_DIM)
        odd = hh % 2
        v_pair = v_ref[:, (hh - odd) * V_DIM:(hh - odd + 2) * V_DIM]
        own = (v_lane >= V_DIM) if odd else (v_lane < V_DIM)
        v_ext = jnp.where(own, v_pair, jnp.ones((), v_pair.dtype))
        for qi in reversed(range(seq // tq)):
            q_lo = qi * tq
            q = q_ref[q_lo:q_lo + tq, cols]
            s_diag = lax.dot_general(q, k_ref[q_lo:q_lo + tq, cols], contract_last,
                                     preferred_element_type=F32)
            s_diag = jnp.where(diag_mask, s_diag, -jnp.inf)
            m = jnp.max(s_diag, axis=-1, keepdims=True)
            if qi:
                s_past = lax.dot_general(q, k_ref[0:q_lo, cols], contract_last,
                                         preferred_element_type=F32)
                m = jnp.maximum(m, jnp.max(s_past, axis=-1, keepdims=True))
            acc = _dot(jnp.exp2(s_diag - m).astype(BF16), v_ext[q_lo:q_lo + tq, :])
            if qi:
                acc = acc + _dot(jnp.exp2(s_past - m).astype(BF16), v_ext[0:q_lo, :])
            out = (acc / pltpu.roll(acc, V_DIM, axis=1)).astype(o_ref.dtype)
            o_ref[q_lo:q_lo + tq, o_cols] = out[:, odd * V_DIM:(odd + 1) * V_DIM]


def _attention(q, k, v, batch, seq, to_cast=()):
    t = q.shape[0]
    hps = HEADS_PER_STEP
    groups = MLA_HEADS // hps
    steps = batch * groups
    slabs = [w.reshape(-1, w.shape[-1]) for w in to_cast]
    slab_specs = [pl.BlockSpec((s.shape[0] // steps, s.shape[1]), lambda b, p: (b * groups + p, 0))
                  for s in slabs]
    outs = pl.pallas_call(
        functools.partial(_attn_kernel, TQ, len(slabs)),
        out_shape=[jax.ShapeDtypeStruct((t, MLA_HEADS * V_DIM), BF16)]
        + [jax.ShapeDtypeStruct(s.shape, BF16) for s in slabs],
        grid=(batch, groups),
        in_specs=[pl.BlockSpec((seq, hps * HEAD_PAD), lambda b, p: (b, p)),
                  pl.BlockSpec((seq, hps * HEAD_PAD), lambda b, p: (b, p)),
                  pl.BlockSpec((seq, hps * V_DIM), lambda b, p: (b, p))] + slab_specs,
        out_specs=[pl.BlockSpec((seq, hps * V_DIM), lambda b, p: (b, p))] + slab_specs,
        compiler_params=pltpu.CompilerParams(
            dimension_semantics=("parallel", "parallel"), vmem_limit_bytes=VMEM_LIMIT),
        name="chunk_attention",
    )(q, k, v, *slabs)
    return outs[0], [o.reshape(w.shape) for o, w in zip(outs[1:], to_cast)]


def _merge_body(o_ref, gaya_ref, gb_ref, h_ref, wo_ref, wmix_ref, gffn_ref, hmid_ref):
    y_b = _dot(o_ref[...], wo_ref[...])
    merged = gaya_ref[...].astype(F32) + gb_ref[...].astype(F32) * y_b
    h_mid = h_ref[...] + _dot(merged.astype(BF16), wmix_ref[...])
    hmid_ref[...] = h_mid
    return _rms(h_mid, gffn_ref[...])


def _merge_kernel(o_ref, gaya_ref, gb_ref, h_ref, wo_ref, wmix_ref, gffn_ref, hmid_ref, xn_ref):
    xn = _merge_body(o_ref, gaya_ref, gb_ref, h_ref, wo_ref, wmix_ref, gffn_ref, hmid_ref)
    xn_ref[...] = xn.astype(BF16)


R_W1, R_W2, R_E1, R_E2, R_RANK1, R_RANK2 = range(6)


def _merge_router_kernel(o_ref, gaya_ref, gb_ref, h_ref, wo_ref, wmix_ref, gffn_ref,
                         wr_ref, hmid_ref, xp_ref, route_ref, route_t_ref, counts_ref, zeros_ref):
    xn = _merge_body(o_ref, gaya_ref, gb_ref, h_ref, wo_ref, wmix_ref, gffn_ref, hmid_ref)
    tm = xn.shape[0]
    zeros_ref[...] = jnp.zeros(zeros_ref.shape, zeros_ref.dtype)

    @pl.when(pl.program_id(0) == 0)
    def _():
        counts_ref[...] = jnp.zeros(counts_ref.shape, F32)

    x_hi = xn.astype(BF16)
    xp_ref[...] = x_hi.astype(F32)

    x_lo = (xn - x_hi.astype(F32)).astype(BF16)
    prod = _dot(jnp.concatenate([x_hi, x_lo], axis=0), wr_ref[...])
    logits = prod[:tm, :LANES] + (prod[:tm, LANES:] + prod[tm:, :LANES])
    lane = lax.broadcasted_iota(jnp.int32, logits.shape, 1)
    lg = jnp.where(lane < N_EXPERTS, logits, -jnp.inf)
    m1 = jnp.max(lg, axis=-1, keepdims=True)
    i1 = jnp.min(jnp.where(lg == m1, lane, LANES), axis=-1, keepdims=True)
    lg2 = jnp.where(lane == i1, -jnp.inf, lg)
    m2 = jnp.max(lg2, axis=-1, keepdims=True)
    i2 = jnp.min(jnp.where(lg2 == m2, lane, LANES), axis=-1, keepdims=True)
    e2 = jnp.exp(m2 - m1)
    den = 1.0 + e2

    onehot = jnp.where(lane == i1, 1.0, jnp.where(lane == i2, 1.0, 0.0))
    rows = lax.broadcasted_iota(jnp.int32, (tm, tm), 0)
    cols = lax.broadcasted_iota(jnp.int32, (tm, tm), 1)
    tri = jnp.where(rows > cols, 1.0, 0.0).astype(BF16)
    pos = _dot(tri, onehot.astype(BF16)) + counts_ref[...]
    rank1 = jnp.sum(jnp.where(lane == i1, pos, 0.0), axis=-1, keepdims=True)
    rank2 = jnp.sum(jnp.where(lane == i2, pos, 0.0), axis=-1, keepdims=True)
    counts_ref[...] += jnp.sum(onehot, axis=0, keepdims=True)

    record = jnp.zeros(logits.shape, F32)
    for lane_id, val in ((R_W1, 1.0 / den), (R_W2, e2 / den), (R_E1, i1.astype(F32)),
                         (R_E2, i2.astype(F32)), (R_RANK1, rank1), (R_RANK2, rank2)):
        record = jnp.where(lane == lane_id, val, record)
    route_ref[...] = record
    route_t_ref[...] = jnp.transpose(record)[:SUBLANES, :]


def _sorted_rows(t):
    return (TOP_K * t // TM_GROUP + N_EXPERTS) * TM_GROUP


def _merge(o, gaya, gb, h, layer, wo, wmix, gffn, router=None):
    t = h.shape[0]
    tm = TM_MIX
    row = lambda w: pl.BlockSpec((tm, w), lambda i: (i, 0))
    in_specs = [row(D_MODEL)] * 4 + [_const_spec(wo.shape, layer=layer),
                                     _const_spec(wmix.shape, layer=layer),
                                     _const_spec(gffn.shape)]
    args = [o, gaya, gb, h, wo, wmix, gffn]
    if router is None:
        body = _merge_kernel
        out_shape = [jax.ShapeDtypeStruct((t, D_MODEL), F32),
                     jax.ShapeDtypeStruct((t, D_MODEL), BF16)]
        out_specs = [row(D_MODEL), row(D_MODEL)]
    else:
        body = _merge_router_kernel
        in_specs += [_const_spec(router.shape)]
        args += [router]
        out_shape = [jax.ShapeDtypeStruct((t, D_MODEL), F32),
                     jax.ShapeDtypeStruct((t, D_MODEL), F32),
                     jax.ShapeDtypeStruct((t, LANES), F32),
                     jax.ShapeDtypeStruct((SUBLANES, t), F32),
                     jax.ShapeDtypeStruct((1, LANES), F32),
                     jax.ShapeDtypeStruct((_sorted_rows(t), D_MODEL), F32)]
        out_specs = [row(D_MODEL), row(D_MODEL), row(LANES),
                     pl.BlockSpec((SUBLANES, tm), lambda i: (0, i)), _const_spec((1, LANES)),
                     pl.BlockSpec((_sorted_rows(t) // (t // tm), D_MODEL), lambda i: (i, 0))]
    return pl.pallas_call(
        body, out_shape=out_shape, grid=(t // tm,), in_specs=in_specs, out_specs=out_specs,
        compiler_params=pltpu.CompilerParams(
            dimension_semantics=("arbitrary",), vmem_limit_bytes=VMEM_LIMIT),
        name="merge_router" if router is not None else "merge",
    )(*args)


def _swiglu(x, wg, wu, wd):
    g = _dot(x, wg)
    u = _dot(x, wu)
    return _dot((g * jax.nn.sigmoid(g) * u).astype(BF16), wd)


def _dense_ffn_kernel(x_ref, hmid_ref, wg_ref, wu_ref, wd_ref, out_ref):
    out_ref[...] = hmid_ref[...] + _swiglu(x_ref[...], wg_ref[...], wu_ref[...], wd_ref[...])


def _dense_ffn(xn, h_mid, wg, wu, wd):
    t = xn.shape[0]
    tm = TM_FFN
    row = pl.BlockSpec((tm, D_MODEL), lambda i: (i, 0))
    return pl.pallas_call(
        _dense_ffn_kernel,
        out_shape=jax.ShapeDtypeStruct((t, D_MODEL), F32),
        grid=(t // tm,),
        in_specs=[row, row] + [_const_spec(w.shape, single_buffer=True) for w in (wg, wu, wd)],
        out_specs=row,
        compiler_params=pltpu.CompilerParams(
            dimension_semantics=("parallel",), vmem_limit_bytes=VMEM_LIMIT),
        name="dense_ffn",
    )(xn, h_mid, wg, wu, wd)


def _row_copy(src_ref, src_row, dst_ref, dst_row, sem):
    return pltpu.make_async_copy(src_ref.at[pl.ds(src_row, 1)], dst_ref.at[pl.ds(dst_row, 1)], sem)


def _for_each_row(tm, fn):
    def body(r, carry):
        fn(r)
        return carry
    lax.fori_loop(0, tm, body, 0, unroll=DMA_UNROLL)


def _drain_row_copies(src_ref, dst_ref, sem, n):
    def body(_, carry):
        _row_copy(src_ref, 0, dst_ref, 0, sem).wait()
        return carry
    lax.fori_loop(0, n, body, 0, unroll=DMA_UNROLL)


def _dispatch_kernel(slots_ref, x_ref, _, xs_ref, sem):
    tm = x_ref.shape[0]

    def issue(r):
        for k in range(TOP_K):
            _row_copy(x_ref, r, xs_ref, slots_ref[0, 0, TOP_K * r + k], sem).start(priority=k % 2)

    _for_each_row(tm, issue)
    _drain_row_copies(x_ref, xs_ref, sem, TOP_K * tm)


def _dispatch(xp, slots, xs0):
    t, w = xp.shape
    tm = TM_ROUTE
    return pl.pallas_call(
        _dispatch_kernel,
        out_shape=jax.ShapeDtypeStruct(xs0.shape, xp.dtype),
        grid=(t // tm,),
        in_specs=[pl.BlockSpec((1, 1, TOP_K * tm), lambda i: (i, 0, 0), memory_space=pltpu.SMEM),
                  pl.BlockSpec((tm, w), lambda i: (i, 0)),
                  pl.BlockSpec(memory_space=pl.ANY)],
        out_specs=pl.BlockSpec(memory_space=pl.ANY),
        scratch_shapes=[pltpu.SemaphoreType.DMA],
        input_output_aliases={2: 0},
        compiler_params=pltpu.CompilerParams(dimension_semantics=("arbitrary",)),
        name="moe_dispatch",
    )(slots, xp, xs0)


def _group_ffn_kernel(te_ref, tb_ref, nv_ref, x_ref, wg_ref, wu_ref, wd_ref, y_ref):
    del te_ref, tb_ref
    live = pl.program_id(0) < nv_ref[0]

    @pl.when(live)
    def _():
        y_ref[...] = _swiglu(x_ref[...].astype(BF16), wg_ref[...], wu_ref[...], wd_ref[...])

    @pl.when(jnp.logical_not(live))
    def _():
        y_ref[...] = jnp.zeros(y_ref.shape, y_ref.dtype)


def _group_ffn(xs, tile_expert, tile_block, n_valid, wg, wu, wd):
    n_rows, w = xs.shape
    tm = TM_GROUP
    _, _, d_ff = wg.shape
    return pl.pallas_call(
        _group_ffn_kernel,
        out_shape=jax.ShapeDtypeStruct((n_rows, D_MODEL), F32),
        grid_spec=pltpu.PrefetchScalarGridSpec(
            num_scalar_prefetch=3,
            grid=(n_rows // tm,),
            in_specs=[pl.BlockSpec((tm, w), lambda i, te, tb, nv: (tb[i], 0)),
                      pl.BlockSpec((None, D_MODEL, d_ff), lambda i, te, tb, nv: (te[i], 0, 0)),
                      pl.BlockSpec((None, D_MODEL, d_ff), lambda i, te, tb, nv: (te[i], 0, 0)),
                      pl.BlockSpec((None, d_ff, D_MODEL), lambda i, te, tb, nv: (te[i], 0, 0))],
            out_specs=pl.BlockSpec((tm, D_MODEL), lambda i, te, tb, nv: (i, 0))),
        compiler_params=pltpu.CompilerParams(
            dimension_semantics=("arbitrary",), vmem_limit_bytes=VMEM_LIMIT),
        name="moe_group_ffn",
    )(tile_expert, tile_block, n_valid, xs, wg, wu, wd)


def _combine_kernel(final_norm, slots_ref, route_ref, hmid_ref, gfin_ref, ys_ref, out_ref,
                    ybuf, sem):
    tm = hmid_ref.shape[0]

    def issue(r):
        for k in range(TOP_K):
            _row_copy(ys_ref, slots_ref[0, 0, TOP_K * r + k], ybuf.at[k], r, sem).start(
                priority=k % 2)

    _for_each_row(tm, issue)
    _drain_row_copies(ys_ref, ybuf.at[0], sem, TOP_K * tm)
    route = route_ref[...]
    h = (hmid_ref[...] + route[:, R_W1:R_W1 + 1] * ybuf[0]) + route[:, R_W2:R_W2 + 1] * ybuf[1]
    out_ref[...] = _rms(h, gfin_ref[...]) if final_norm else h


def _combine(ys, slots, route, h_mid, gfin, final_norm):
    t = h_mid.shape[0]
    tm = TM_ROUTE
    row = lambda w: pl.BlockSpec((tm, w), lambda i: (i, 0))
    return pl.pallas_call(
        functools.partial(_combine_kernel, final_norm),
        out_shape=jax.ShapeDtypeStruct((t, D_MODEL), F32),
        grid=(t // tm,),
        in_specs=[pl.BlockSpec((1, 1, TOP_K * tm), lambda i: (i, 0, 0), memory_space=pltpu.SMEM),
                  row(LANES), row(D_MODEL), _const_spec(gfin.shape),
                  pl.BlockSpec(memory_space=pl.ANY)],
        out_specs=row(D_MODEL),
        scratch_shapes=[pltpu.VMEM((TOP_K, tm, D_MODEL), F32), pltpu.SemaphoreType.DMA],
        compiler_params=pltpu.CompilerParams(dimension_semantics=("arbitrary",)),
        name="moe_combine",
    )(slots, route, h_mid, gfin, ys)


def _moe_ffn(xp, route, route_t, counts, xs0, h_mid, wg, wu, wd, gfin, final_norm):
    t = h_mid.shape[0]
    tm = TM_GROUP
    n_tiles = xs0.shape[0] // tm
    tiles_per_expert = (counts[0, :N_EXPERTS].astype(jnp.int32) + tm - 1) // tm
    tile_end = jnp.cumsum(tiles_per_expert)
    row_start = (tile_end - tiles_per_expert) * tm
    experts = jnp.arange(N_EXPERTS, dtype=jnp.int32)

    def slot(expert_row, rank_row):
        expert = route_t[expert_row].astype(jnp.int32)
        start = jnp.sum(jnp.where(expert[:, None] == experts[None, :], row_start[None, :], 0), axis=1)
        return start + route_t[rank_row].astype(jnp.int32)

    slots = jnp.stack([slot(R_E1, R_RANK1), slot(R_E2, R_RANK2)], axis=1)
    slots = slots.reshape(t // TM_ROUTE, 1, TOP_K * TM_ROUTE)
    n_valid = tile_end[-1:]
    tile_block = jnp.minimum(jnp.arange(n_tiles, dtype=jnp.int32), n_valid - 1)
    tile_expert = jnp.sum((tile_block[:, None] >= tile_end[None, :]).astype(jnp.int32), axis=1)
    tile_expert = jnp.minimum(tile_expert, N_EXPERTS - 1)
    xs = _dispatch(xp, slots, xs0)
    ys = _group_ffn(xs, tile_expert, tile_block, n_valid, wg, wu, wd)
    return _combine(ys, slots, route, h_mid, gfin, final_norm)


def _final_norm_kernel(h_ref, g_ref, out_ref):
    out_ref[...] = _rms(h_ref[...], g_ref[...])


def _final_norm(h, g):
    t = h.shape[0]
    row = pl.BlockSpec((TM_FFN, D_MODEL), lambda i: (i, 0))
    return pl.pallas_call(
        _final_norm_kernel, out_shape=jax.ShapeDtypeStruct((t, D_MODEL), F32),
        grid=(t // TM_FFN,), in_specs=[row, _const_spec(g.shape)], out_specs=row,
        name="final_norm",
    )(h, g)


def _pack_w_in(w):
    d = w.shape[0]
    kr_a = w[:, C_KROPE:C_KROPE + HALF_ROPE]
    kr_b = w[:, C_KROPE + HALF_ROPE:C_KROPE + QK_ROPE]
    z_lo = jnp.zeros((d, QK_NOPE), w.dtype)
    z_hi = jnp.zeros((d, HEAD_PAD - QK_NOPE - QK_ROPE), w.dtype)
    w_kr = jnp.concatenate([z_lo, kr_a, kr_b, z_hi, z_lo, kr_b, kr_a, z_hi], axis=1)
    return (w[:, :C_KROPE].astype(BF16), w_kr.astype(BF16), w[:, C_KROPE + QK_ROPE:].astype(BF16))


def _pack_w_uq(w):
    w3 = w.reshape(Q_LORA, MLA_HEADS, QK_NOPE + QK_ROPE)
    nope = w3[..., :QK_NOPE]
    r_a = w3[..., QK_NOPE:QK_NOPE + HALF_ROPE]
    r_b = w3[..., QK_NOPE + HALF_ROPE:]
    pad = jnp.zeros((Q_LORA, MLA_HEADS, HEAD_PAD - QK_NOPE - QK_ROPE), w.dtype)
    plain = jnp.concatenate([nope, r_a, r_b, pad], axis=-1)
    swapped = jnp.concatenate([jnp.zeros_like(nope), r_b, r_a, pad], axis=-1)
    return (plain.reshape(Q_LORA, MLA_HEADS * HEAD_PAD),
            swapped.reshape(Q_LORA, MLA_HEADS * HEAD_PAD))


def _pack_w_ukv(w):
    w3 = w.reshape(KV_LORA, MLA_HEADS, QK_NOPE + V_DIM)
    k = jnp.concatenate([w3[..., :QK_NOPE],
                         jnp.zeros((KV_LORA, MLA_HEADS, HEAD_PAD - QK_NOPE), w.dtype)], axis=-1)
    return (k.reshape(KV_LORA, MLA_HEADS * HEAD_PAD),
            w3[..., QK_NOPE:].reshape(KV_LORA, MLA_HEADS * V_DIM))


def kernel(x, positions, norm_mix, w_in, b_gate, conv_w, w_conv_out, norm_q, norm_kv, w_uq, w_ukv,
           w_attn_out, w_mix_out, norm_ffn, dense_w_gate, dense_w_up, dense_w_down, router_w,
           moe_w_gate, moe_w_up, moe_w_down, norm_final):
    batch, seq, d = x.shape
    t = batch * seq
    c_tab, s_tab, (wco_all, wo_all, wmix_all) = _rope_tables(
        positions, (w_conv_out, w_attn_out, w_mix_out))
    h = x.reshape(t, d)
    expert_w = None
    for layer in range(DEPTH):
        wuq, wuqs = _pack_w_uq(w_uq[layer])
        wuk, wuv = _pack_w_ukv(w_ukv[layer])
        q, k, v, gaya, gb = _front(
            h, seq, layer, norm_mix[layer].reshape(1, d), *_pack_w_in(w_in[layer]),
            b_gate[layer].reshape(1, 2 * d), conv_w[layer].reshape(CONV_K, CONV_WIDTH),
            wco_all, norm_q[layer].reshape(1, Q_LORA),
            norm_kv[layer].reshape(1, KV_LORA), wuq.astype(BF16), wuqs.astype(BF16),
            wuk.astype(BF16), wuv.astype(BF16), c_tab, s_tab)
        i = layer // 2
        last = layer == DEPTH - 1
        this_expert_w = expert_w
        next_is_moe = not last and (layer + 1) % 2 == 1
        nxt = (layer + 1) // 2
        o, expert_w = _attention(
            q, k, v, batch, seq,
            (moe_w_gate[nxt], moe_w_up[nxt], moe_w_down[nxt]) if next_is_moe else ())
        merge_args = (o, gaya, gb, h, layer, wo_all, wmix_all, norm_ffn[layer].reshape(1, d))
        if layer % 2 == 0:
            h_mid, xn = _merge(*merge_args)
            h = _dense_ffn(xn, h_mid, dense_w_gate[i].astype(BF16), dense_w_up[i].astype(BF16),
                           dense_w_down[i].astype(BF16))
            if last:
                h = _final_norm(h, norm_final.reshape(1, d))
        else:
            wr = jnp.pad(router_w[i], ((0, 0), (0, LANES - N_EXPERTS)))
            wr_hi = wr.astype(BF16)
            wr_lo = (wr - wr_hi.astype(F32)).astype(BF16)
            h_mid, xp, route, route_t, counts, xs0 = _merge(
                *merge_args, router=jnp.concatenate([wr_hi, wr_lo], axis=1))
            wg, wu, wd = this_expert_w
            h = _moe_ffn(xp, route, route_t, counts, xs0, h_mid, wg, wu, wd,
                         norm_final.reshape(1, d), last)
    return h.reshape(batch, seq, d)
```
